```python
import jax, jax.numpy as jnp
from jax import lax
import numpy as np

D_MODEL = 1024
BATCH = 8
SEQ = 4096
DEPTH = 1

D_FF = 2816
D_PLE = 256
D_GMLP = D_MODEL
N_SGU_GROUPS = 4
CHUNK = 128
D_POOL = D_MODEL
POOL_WINDOWS = (2, 4, 8, 16)
N_POOL_GROUPS = len(POOL_WINDOWS)
D_IN = 2 * D_GMLP + D_POOL + 2 * D_MODEL
EPS = 1e-6

kernel_name = "hybrid_sgu_pool_macaron_layer"


def _rmsnorm(x, g):
    xf = x.astype(jnp.float32)
    y = xf * lax.rsqrt(jnp.mean(xf * xf, axis=-1, keepdims=True) + EPS)
    return (y * g.astype(jnp.float32)).astype(x.dtype)


def _layernorm(x, g):
    xf = x.astype(jnp.float32)
    mu = jnp.mean(xf, axis=-1, keepdims=True)
    xc = xf - mu
    y = xc * lax.rsqrt(jnp.mean(xc * xc, axis=-1, keepdims=True) + EPS)
    return (y * g.astype(jnp.float32)).astype(x.dtype)


def _swiglu(xn, w_gate, w_up, w_down):
    return (jax.nn.silu(xn @ w_gate) * (xn @ w_up)) @ w_down


def _spatial_gating(u, v, norm_g, w_s, b_s):
    B, S, _ = v.shape
    dg = D_GMLP // N_SGU_GROUPS
    v = _layernorm(v, norm_g)
    vc = v.reshape(B, S // CHUNK, CHUNK, N_SGU_GROUPS, dg)
    causal = jnp.tril(jnp.ones((CHUNK, CHUNK), dtype=bool))
    ws = jnp.where(causal[None], w_s, 0.0).astype(v.dtype)
    sv = jnp.einsum('gts,bcsgd->bctgd', ws, vc) + b_s.T[:, :, None].astype(v.dtype)
    return u * sv.reshape(B, S, D_GMLP)


def _pool_mixer(c, pool_w, pool_scale):
    B, S, _ = c.shape
    dg = D_POOL // N_POOL_GROUPS
    cf = c.astype(jnp.float32)
    cs = jnp.concatenate([jnp.zeros((B, 1, D_POOL), jnp.float32), jnp.cumsum(cf, axis=1)], axis=1)
    t = jnp.arange(S)
    outs = []
    for gi, w in enumerate(POOL_WINDOWS):
        lo = jnp.maximum(t + 1 - w, 0)
        sl = slice(gi * dg, (gi + 1) * dg)
        csg = cs[:, :, sl]
        count = (t + 1 - lo).astype(jnp.float32)[None, :, None]
        mean = (csg[:, 1:] - csg[:, lo]) / count
        diff = (mean - cf[:, :, sl]).astype(c.dtype)
        outs.append(jnp.einsum('bsc,cd->bsd', diff, pool_w[gi]))
    return jnp.concatenate(outs, axis=-1) * pool_scale


def _token_mixer(xn, w_in, sgu_norm_g, sgu_w, sgu_b, pool_w, pool_scale, w_out_a, w_out_b, w_o):
    z = xn @ w_in
    i1 = D_GMLP
    i2 = 2 * D_GMLP
    i3 = i2 + D_POOL
    i4 = i3 + D_MODEL
    u = jax.nn.gelu(z[..., :i1])
    v = jax.nn.gelu(z[..., i1:i2])
    c = z[..., i2:i3]
    ga = z[..., i3:i4]
    gb = z[..., i4:]
    a = _spatial_gating(u, v, sgu_norm_g, sgu_w, sgu_b)
    b = _pool_mixer(c, pool_w, pool_scale)
    y = jax.nn.sigmoid(ga) * (a @ w_out_a) + jax.nn.sigmoid(gb) * (b @ w_out_b)
    return y @ w_o


def _gain(k, shape):
    return 1.0 + 0.02 * jax.random.normal(k, shape, jnp.float32)


def _w(k, shape, fan_in):
    return jax.random.normal(k, shape, jnp.float32) * (fan_in ** -0.5)


def setup_inputs(seed: int = 0) -> dict:
    key = jax.random.key(seed)
    ks = jax.random.split(key, 32)
    L = DEPTH
    dgp = D_POOL // N_POOL_GROUPS
    return {
        "x": jax.random.normal(ks[0], (BATCH, SEQ, D_MODEL), jnp.float32),
        "p": jax.random.normal(ks[1], (DEPTH, BATCH, SEQ, D_PLE), jnp.float32),
        "ffn1_pre_g": _gain(ks[2], (L, D_MODEL)),
        "ffn1_w_gate": _w(ks[3], (L, D_MODEL, D_FF), D_MODEL),
        "ffn1_w_up": _w(ks[4], (L, D_MODEL, D_FF), D_MODEL),
        "ffn1_w_down": _w(ks[5], (L, D_FF, D_MODEL), D_FF),
        "ffn1_post_g": _gain(ks[6], (L, D_MODEL)),
        "mix_pre_g": _gain(ks[7], (L, D_MODEL)),
        "w_in": _w(ks[8], (L, D_MODEL, D_IN), D_MODEL),
        "sgu_norm_g": _gain(ks[9], (L, D_GMLP)),
        "sgu_w": _w(ks[10], (L, N_SGU_GROUPS, CHUNK, CHUNK), CHUNK),
        "sgu_b": _gain(ks[11], (L, N_SGU_GROUPS, CHUNK)),
        "pool_w": _w(ks[12], (L, N_POOL_GROUPS, dgp, dgp), dgp),
        "pool_scale": _gain(ks[13], (L, D_POOL)),
        "w_out_a": _w(ks[14], (L, D_GMLP, D_MODEL), D_GMLP),
        "w_out_b": _w(ks[15], (L, D_POOL, D_MODEL), D_POOL),
        "w_o": _w(ks[16], (L, D_MODEL, D_MODEL), D_MODEL),
        "mix_post_g": _gain(ks[17], (L, D_MODEL)),
        "ffn2_pre_g": _gain(ks[18], (L, D_MODEL)),
        "ffn2_w_gate": _w(ks[19], (L, D_MODEL, D_FF), D_MODEL),
        "ffn2_w_up": _w(ks[20], (L, D_MODEL, D_FF), D_MODEL),
        "ffn2_w_down": _w(ks[21], (L, D_FF, D_MODEL), D_FF),
        "ffn2_post_g": _gain(ks[22], (L, D_MODEL)),
        "ple_pre_g": _gain(ks[23], (L, D_MODEL)),
        "ple_w_gate": _w(ks[24], (L, D_MODEL, D_MODEL), D_MODEL),
        "ple_w_proj": _w(ks[25], (L, D_PLE, D_MODEL), D_PLE),
        "ple_post_g": _gain(ks[26], (L, D_MODEL)),
    }


def reference(x, p, ffn1_pre_g, ffn1_w_gate, ffn1_w_up, ffn1_w_down, ffn1_post_g,
              mix_pre_g, w_in, sgu_norm_g, sgu_w, sgu_b, pool_w, pool_scale,
              w_out_a, w_out_b, w_o, mix_post_g,
              ffn2_pre_g, ffn2_w_gate, ffn2_w_up, ffn2_w_down, ffn2_post_g,
              ple_pre_g, ple_w_gate, ple_w_proj, ple_post_g):
    h = x
    for i in range(DEPTH):
        f = _swiglu(_rmsnorm(h, ffn1_pre_g[i]), ffn1_w_gate[i], ffn1_w_up[i], ffn1_w_down[i])
        h = h + 0.5 * _rmsnorm(f, ffn1_post_g[i])
        m = _token_mixer(_rmsnorm(h, mix_pre_g[i]), w_in[i], sgu_norm_g[i], sgu_w[i], sgu_b[i],
                         pool_w[i], pool_scale[i], w_out_a[i], w_out_b[i], w_o[i])
        h = h + _rmsnorm(m, mix_post_g[i])
        f = _swiglu(_rmsnorm(h, ffn2_pre_g[i]), ffn2_w_gate[i], ffn2_w_up[i], ffn2_w_down[i])
        h = h + 0.5 * _rmsnorm(f, ffn2_post_g[i])
        gate = jax.nn.sigmoid(_rmsnorm(h, ple_pre_g[i]) @ ple_w_gate[i])
        e = p[i] @ ple_w_proj[i]
        h = h + _rmsnorm(gate * e, ple_post_g[i])
    return h
```

```python
import functools

import jax
import jax.numpy as jnp
from jax import lax
from jax.experimental import pallas as pl
from jax.experimental.pallas import tpu as pltpu

EPS = 1e-6
CHUNK = 128
N_SGU_GROUPS = 4
POOL_WINDOWS = (2, 4, 8, 16)
HALO = 16
TOKEN_TILE = 512
V7X_VMEM_LIMIT_BYTES = 58 * 1024 * 1024

_BF16 = jnp.bfloat16
_F32 = jnp.float32


def _rms(x, g):
    return x * lax.rsqrt(jnp.mean(x * x, axis=-1, keepdims=True) + EPS) * g


def _dot(a, b):
    return jnp.dot(a, b, preferred_element_type=_F32)


def _swiglu_half_step(h, pre_g, post_g, wg_ref, wu_ref, wd_ref):
    xn = _rms(h, pre_g).astype(_BF16)
    gate = _dot(xn, wg_ref[...])
    up = _dot(xn, wu_ref[...])
    act = (jax.nn.silu(gate) * up).astype(_BF16)
    f = _dot(act, wd_ref[...])
    return h + 0.5 * _rms(f, post_g)


def _ffn_kernel(h_ref, pre_g_ref, post_g_ref, wg_ref, wu_ref, wd_ref, o_ref):
    o_ref[...] = _swiglu_half_step(h_ref[...], pre_g_ref[...], post_g_ref[...], wg_ref, wu_ref, wd_ref)


def _ffn_ple_kernel(h_ref, p_ref, pre_g_ref, post_g_ref, wg_ref, wu_ref, wd_ref,
                    ple_pre_g_ref, ple_wg_ref, ple_wp_ref, ple_post_g_ref, o_ref):
    h = _swiglu_half_step(h_ref[...], pre_g_ref[...], post_g_ref[...], wg_ref, wu_ref, wd_ref)
    gate = jax.nn.sigmoid(_dot(_rms(h, ple_pre_g_ref[...]).astype(_BF16), ple_wg_ref[...]))
    e = _dot(p_ref[...].astype(_BF16), ple_wp_ref[...])
    o_ref[...] = h + _rms(gate * e, ple_post_g_ref[...])


def _mixer_kernel(h_ref, pre_g_ref, w_in_ref, sgu_g_ref, sgu_w_ref, sgu_bt_ref, pool_w_ref, pool_scale_ref,
                  w_out_a_ref, w_out_b_ref, w_o_ref, post_g_ref, o_ref, a_scr, b_scr, cext_scr):
    tm, d = h_ref.shape
    dg = d // N_SGU_GROUPS
    j = pl.program_id(1)

    h = h_ref[...]
    xn = _rms(h, pre_g_ref[...]).astype(_BF16)

    u = jax.nn.gelu(_dot(xn, w_in_ref[:, 0:d]))
    v = jax.nn.gelu(_dot(xn, w_in_ref[:, d:2 * d]))
    mu = jnp.mean(v, axis=-1, keepdims=True)
    vc = v - mu
    v = (vc * lax.rsqrt(jnp.mean(vc * vc, axis=-1, keepdims=True) + EPS) * sgu_g_ref[...]).astype(_BF16)
    row = lax.broadcasted_iota(jnp.int32, (CHUNK, CHUNK), 0)
    col = lax.broadcasted_iota(jnp.int32, (CHUNK, CHUNK), 1)
    for g in range(N_SGU_GROUPS):
        ws = jnp.where(row >= col, sgu_w_ref[g], 0.0).astype(_BF16)
        bias = sgu_bt_ref[:, g:g + 1]
        for ci in range(tm // CHUNK):
            rs = slice(ci * CHUNK, (ci + 1) * CHUNK)
            cs = slice(g * dg, (g + 1) * dg)
            sv = _dot(ws, v[rs, cs]) + bias
            a_scr[rs, cs] = (u[rs, cs] * sv).astype(_BF16)

    c = _dot(xn, w_in_ref[:, 2 * d:3 * d])

    @pl.when(j == 0)
    def _():
        cext_scr[0:HALO, :] = jnp.zeros((HALO, d), _F32)

    cext_scr[HALO:HALO + tm, :] = c
    t = j * tm + lax.broadcasted_iota(jnp.int32, (tm, 1), 0)
    dp = d // len(POOL_WINDOWS)
    for gi, w in enumerate(POOL_WINDOWS):
        cs = slice(gi * dp, (gi + 1) * dp)
        s = cext_scr[HALO:HALO + tm, cs]
        for k in range(1, w):
            s = s + cext_scr[HALO - k:HALO - k + tm, cs]
        count = jnp.minimum(t + 1, w).astype(_F32)
        diff = (s / count - c[:, cs]).astype(_BF16)
        b_scr[:, cs] = (_dot(diff, pool_w_ref[gi]) * pool_scale_ref[:, cs]).astype(_BF16)
    cext_scr[0:HALO, :] = cext_scr[tm:tm + HALO, :]

    ga = _dot(xn, w_in_ref[:, 3 * d:4 * d])
    gb = _dot(xn, w_in_ref[:, 4 * d:5 * d])
    y = (jax.nn.sigmoid(ga) * _dot(a_scr[...], w_out_a_ref[...])
         + jax.nn.sigmoid(gb) * _dot(b_scr[...], w_out_b_ref[...]))
    m = _dot(y.astype(_BF16), w_o_ref[...])
    o_ref[...] = h + _rms(m, post_g_ref[...])


def _resident():
    return pl.BlockSpec(memory_space=pltpu.VMEM)


def _params(n_axes):
    return pltpu.CompilerParams(dimension_semantics=("arbitrary",) * n_axes,
                                vmem_limit_bytes=V7X_VMEM_LIMIT_BYTES)


def _ffn_call(h, pre_g, post_g, wg, wu, wd):
    n, d = h.shape
    tile = pl.BlockSpec((TOKEN_TILE, d), lambda i: (i, 0))
    return pl.pallas_call(
        _ffn_kernel,
        grid=(n // TOKEN_TILE,),
        in_specs=[tile] + [_resident()] * 5,
        out_specs=tile,
        out_shape=jax.ShapeDtypeStruct((n, d), _F32),
        compiler_params=_params(1),
        name="ffn1",
    )(h, pre_g, post_g, wg, wu, wd)


def _ffn_ple_call(h, p, pre_g, post_g, wg, wu, wd, ple_pre_g, ple_wg, ple_wp, ple_post_g):
    n, d = h.shape
    tile = pl.BlockSpec((TOKEN_TILE, d), lambda i: (i, 0))
    p_tile = pl.BlockSpec((TOKEN_TILE, p.shape[1]), lambda i: (i, 0))
    return pl.pallas_call(
        _ffn_ple_kernel,
        grid=(n // TOKEN_TILE,),
        in_specs=[tile, p_tile] + [_resident()] * 9,
        out_specs=tile,
        out_shape=jax.ShapeDtypeStruct((n, d), _F32),
        compiler_params=_params(1),
        name="ffn2_ple",
    )(h, p, pre_g, post_g, wg, wu, wd, ple_pre_g, ple_wg, ple_wp, ple_post_g)


def _mixer_call(h, batch, seq, pre_g, w_in, sgu_g, sgu_w, sgu_bt, pool_w, pool_scale, w_out_a, w_out_b, w_o,
                post_g):
    n, d = h.shape
    tiles_per_seq = seq // TOKEN_TILE
    tile = pl.BlockSpec((TOKEN_TILE, d), lambda b, j: (b * tiles_per_seq + j, 0))
    return pl.pallas_call(
        _mixer_kernel,
        grid=(batch, tiles_per_seq),
        in_specs=[tile] + [_resident()] * 11,
        out_specs=tile,
        out_shape=jax.ShapeDtypeStruct((n, d), _F32),
        scratch_shapes=[
            pltpu.VMEM((TOKEN_TILE, d), _BF16),
            pltpu.VMEM((TOKEN_TILE, d), _BF16),
            pltpu.VMEM((TOKEN_TILE + HALO, d), _F32),
        ],
        compiler_params=_params(2),
        name="mixer",
    )(h, pre_g, w_in, sgu_g, sgu_w, sgu_bt, pool_w, pool_scale, w_out_a, w_out_b, w_o, post_g)


def kernel(x, p, ffn1_pre_g, ffn1_w_gate, ffn1_w_up, ffn1_w_down, ffn1_post_g, mix_pre_g, w_in, sgu_norm_g, sgu_w, sgu_b, pool_w, pool_scale, w_out_a, w_out_b, w_o, mix_post_g, ffn2_pre_g, ffn2_w_gate, ffn2_w_up, ffn2_w_down, ffn2_post_g, ple_pre_g, ple_w_gate, ple_w_proj, ple_post_g):
    batch, seq, d = x.shape
    depth = p.shape[0]
    assert seq % TOKEN_TILE == 0 and TOKEN_TILE % CHUNK == 0
    bf = lambda w: w.astype(_BF16)
    row = lambda g: g.reshape(1, -1)

    h = x.reshape(batch * seq, d)
    for i in range(depth):
        h = _ffn_call(h, row(ffn1_pre_g[i]), row(ffn1_post_g[i]),
                      bf(ffn1_w_gate[i]), bf(ffn1_w_up[i]), bf(ffn1_w_down[i]))
        h = _mixer_call(h, batch, seq, row(mix_pre_g[i]), bf(w_in[i]), row(sgu_norm_g[i]), sgu_w[i],
                        sgu_b[i].T, bf(pool_w[i]), row(pool_scale[i]), bf(w_out_a[i]), bf(w_out_b[i]),
                        bf(w_o[i]), row(mix_post_g[i]))
        h = _ffn_ple_call(h, p[i].reshape(batch * seq, -1), row(ffn2_pre_g[i]), row(ffn2_post_g[i]),
                          bf(ffn2_w_gate[i]), bf(ffn2_w_up[i]), bf(ffn2_w_down[i]),
                          row(ple_pre_g[i]), bf(ple_w_gate[i]), bf(ple_w_proj[i]), row(ple_post_g[i]))
    return h.reshape(batch, seq, d)
```

```python
import functools

import jax
import jax.numpy as jnp
from jax import lax
from jax.experimental import pallas as pl
from jax.experimental.pallas import tpu as pltpu

EPS = 1e-6
CHUNK = 128
N_SGU_GROUPS = 4
POOL_WINDOWS = (2, 4, 8, 16)
HALO = 16
TOKEN_TILE = 512
V7X_VMEM_LIMIT_BYTES = 58 * 1024 * 1024

_BF16 = jnp.bfloat16
_F32 = jnp.float32


def _rms(x, g):
    return x * lax.rsqrt(jnp.mean(x * x, axis=-1, keepdims=True) + EPS) * g


def _dot(a, b):
    return jnp.dot(a, b, preferred_element_type=_F32)


def _swiglu_half_step(h, pre_g, post_g, wg_ref, wu_ref, wd_ref):
    xn = _rms(h, pre_g).astype(_BF16)
    gate = _dot(xn, wg_ref[...])
    up = _dot(xn, wu_ref[...])
    act = (jax.nn.silu(gate) * up).astype(_BF16)
    f = _dot(act, wd_ref[...])
    return h + 0.5 * _rms(f, post_g)


def _ffn_kernel(h_ref, pre_g_ref, post_g_ref, wg_ref, wu_ref, wd_ref, o_ref):
    o_ref[...] = _swiglu_half_step(h_ref[...], pre_g_ref[...], post_g_ref[...], wg_ref, wu_ref, wd_ref)


def _ffn_ple_kernel(h_ref, p_ref, pre_g_ref, post_g_ref, wg_ref, wu_ref, wd_ref,
                    ple_pre_g_ref, ple_wg_ref, ple_wp_ref, ple_post_g_ref, o_ref):
    h = _swiglu_half_step(h_ref[...], pre_g_ref[...], post_g_ref[...], wg_ref, wu_ref, wd_ref)
    gate = jax.nn.sigmoid(_dot(_rms(h, ple_pre_g_ref[...]).astype(_BF16), ple_wg_ref[...]))
    e = _dot(p_ref[...].astype(_BF16), ple_wp_ref[...])
    o_ref[...] = h + _rms(gate * e, ple_post_g_ref[...])


def _mixer_kernel(h_ref, pre_g_ref, w_in_ref, sgu_g_ref, sgu_w_ref, sgu_bt_ref, pool_w_ref, pool_scale_ref,
                  w_out_a_ref, w_out_b_ref, w_o_ref, post_g_ref, o_ref, a_scr, b_scr, cext_scr):
    tm, d = h_ref.shape
    dg = d // N_SGU_GROUPS
    j = pl.program_id(1)

    h = h_ref[...]
    xn = _rms(h, pre_g_ref[...]).astype(_BF16)

    v = jax.nn.gelu(_dot(xn, w_in_ref[:, d:2 * d]))
    c = _dot(xn, w_in_ref[:, 2 * d:3 * d])
    mu = jnp.mean(v, axis=-1, keepdims=True)
    vc = v - mu
    v = (vc * lax.rsqrt(jnp.mean(vc * vc, axis=-1, keepdims=True) + EPS) * sgu_g_ref[...]).astype(_BF16)

    @pl.when(j == 0)
    def _():
        cext_scr[0:HALO, :] = jnp.zeros((HALO, d), _F32)

    cext_scr[HALO:HALO + tm, :] = c
    u = jax.nn.gelu(_dot(xn, w_in_ref[:, 0:d]))
    gb = _dot(xn, w_in_ref[:, 4 * d:5 * d])
    ga = _dot(xn, w_in_ref[:, 3 * d:4 * d])

    row = lax.broadcasted_iota(jnp.int32, (CHUNK, CHUNK), 0)
    col = lax.broadcasted_iota(jnp.int32, (CHUNK, CHUNK), 1)
    for g in range(N_SGU_GROUPS):
        ws = jnp.where(row >= col, sgu_w_ref[g], 0.0).astype(_BF16)
        bias = sgu_bt_ref[:, g:g + 1]
        for ci in range(tm // CHUNK):
            rs = slice(ci * CHUNK, (ci + 1) * CHUNK)
            cs = slice(g * dg, (g + 1) * dg)
            sv = _dot(ws, v[rs, cs]) + bias
            a_scr[rs, cs] = (u[rs, cs] * sv).astype(_BF16)

    t = j * tm + lax.broadcasted_iota(jnp.int32, (tm, 1), 0)
    dp = d // len(POOL_WINDOWS)
    for gi, w in enumerate(POOL_WINDOWS):
        cs = slice(gi * dp, (gi + 1) * dp)
        s = cext_scr[:, cs]
        span = 1
        while span < w:
            s = s + pltpu.roll(s, span, axis=0)
            span *= 2
        count = jnp.minimum(t + 1, w).astype(_F32)
        diff = (s[HALO:HALO + tm] / count - c[:, cs]).astype(_BF16)
        b_scr[:, cs] = (_dot(diff, pool_w_ref[gi]) * pool_scale_ref[:, cs]).astype(_BF16)
    cext_scr[0:HALO, :] = cext_scr[tm:tm + HALO, :]

    yb = jax.nn.sigmoid(gb) * _dot(b_scr[...], w_out_b_ref[...])
    y = jax.nn.sigmoid(ga) * _dot(a_scr[...], w_out_a_ref[...]) + yb
    m = _dot(y.astype(_BF16), w_o_ref[...])
    o_ref[...] = h + _rms(m, post_g_ref[...])


def _resident():
    return pl.BlockSpec(memory_space=pltpu.VMEM)


def _params(n_axes):
    return pltpu.CompilerParams(dimension_semantics=("arbitrary",) * n_axes,
                                vmem_limit_bytes=V7X_VMEM_LIMIT_BYTES)


def _ffn_call(h, pre_g, post_g, wg, wu, wd):
    n, d = h.shape
    tile = pl.BlockSpec((TOKEN_TILE, d), lambda i: (i, 0))
    return pl.pallas_call(
        _ffn_kernel,
        grid=(n // TOKEN_TILE,),
        in_specs=[tile] + [_resident()] * 5,
        out_specs=tile,
        out_shape=jax.ShapeDtypeStruct((n, d), _F32),
        compiler_params=_params(1),
        name="ffn1",
    )(h, pre_g, post_g, wg, wu, wd)


def _ffn_ple_call(h, p, pre_g, post_g, wg, wu, wd, ple_pre_g, ple_wg, ple_wp, ple_post_g):
    n, d = h.shape
    tile = pl.BlockSpec((TOKEN_TILE, d), lambda i: (i, 0))
    p_tile = pl.BlockSpec((TOKEN_TILE, p.shape[1]), lambda i: (i, 0))
    return pl.pallas_call(
        _ffn_ple_kernel,
        grid=(n // TOKEN_TILE,),
        in_specs=[tile, p_tile] + [_resident()] * 9,
        out_specs=tile,
        out_shape=jax.ShapeDtypeStruct((n, d), _F32),
        compiler_params=_params(1),
        name="ffn2_ple",
    )(h, p, pre_g, post_g, wg, wu, wd, ple_pre_g, ple_wg, ple_wp, ple_post_g)


def _mixer_call(h, batch, seq, pre_g, w_in, sgu_g, sgu_w, sgu_bt, pool_w, pool_scale, w_out_a, w_out_b, w_o,
                post_g):
    n, d = h.shape
    tiles_per_seq = seq // TOKEN_TILE
    tile = pl.BlockSpec((TOKEN_TILE, d), lambda b, j: (b * tiles_per_seq + j, 0))
    return pl.pallas_call(
        _mixer_kernel,
        grid=(batch, tiles_per_seq),
        in_specs=[tile] + [_resident()] * 11,
        out_specs=tile,
        out_shape=jax.ShapeDtypeStruct((n, d), _F32),
        scratch_shapes=[
            pltpu.VMEM((TOKEN_TILE, d), _BF16),
            pltpu.VMEM((TOKEN_TILE, d), _BF16),
            pltpu.VMEM((TOKEN_TILE + HALO, d), _F32),
        ],
        compiler_params=_params(2),
        name="mixer",
    )(h, pre_g, w_in, sgu_g, sgu_w, sgu_bt, pool_w, pool_scale, w_out_a, w_out_b, w_o, post_g)


def kernel(x, p, ffn1_pre_g, ffn1_w_gate, ffn1_w_up, ffn1_w_down, ffn1_post_g, mix_pre_g, w_in, sgu_norm_g, sgu_w, sgu_b, pool_w, pool_scale, w_out_a, w_out_b, w_o, mix_post_g, ffn2_pre_g, ffn2_w_gate, ffn2_w_up, ffn2_w_down, ffn2_post_g, ple_pre_g, ple_w_gate, ple_w_proj, ple_post_g):
    batch, seq, d = x.shape
    depth = p.shape[0]
    assert seq % TOKEN_TILE == 0 and TOKEN_TILE % CHUNK == 0
    bf = lambda w: w.astype(_BF16)
    row = lambda g: g.reshape(1, -1)

    h = x.reshape(batch * seq, d)
    for i in range(depth):
        h = _ffn_call(h, row(ffn1_pre_g[i]), row(ffn1_post_g[i]),
                      bf(ffn1_w_gate[i]), bf(ffn1_w_up[i]), bf(ffn1_w_down[i]))
        h = _mixer_call(h, batch, seq, row(mix_pre_g[i]), bf(w_in[i]), row(sgu_norm_g[i]), sgu_w[i],
                        sgu_b[i].T, bf(pool_w[i]), row(pool_scale[i]), bf(w_out_a[i]), bf(w_out_b[i]),
                        bf(w_o[i]), row(mix_post_g[i]))
        h = _ffn_ple_call(h, p[i].reshape(batch * seq, -1), row(ffn2_pre_g[i]), row(ffn2_post_g[i]),
                          bf(ffn2_w_gate[i]), bf(ffn2_w_up[i]), bf(ffn2_w_down[i]),
                          row(ple_pre_g[i]), bf(ple_w_gate[i]), bf(ple_w_proj[i]), row(ple_post_g[i]))
    return h.reshape(batch, seq, d)
```

```python
import functools

import jax
import jax.numpy as jnp
from jax import lax
from jax.experimental import pallas as pl
from jax.experimental.pallas import tpu as pltpu

EPS = 1e-6
CHUNK = 128
N_SGU_GROUPS = 4
POOL_WINDOWS = (2, 4, 8, 16)
HALO = 16
TOKEN_TILE = 512
V7X_VMEM_LIMIT_BYTES = 58 * 1024 * 1024

_BF16 = jnp.bfloat16
_F32 = jnp.float32


def _rms(x, g):
    return x * lax.rsqrt(jnp.mean(x * x, axis=-1, keepdims=True) + EPS) * g


def _dot(a, b):
    return jnp.dot(a, b, preferred_element_type=_F32)


def _ordering_zero(x):
    rows, cols = x.shape
    acc = x[0:8, :]
    for r in range(1, rows // 8):
        acc = jnp.maximum(acc, x[8 * r:8 * r + 8, :])
    red = acc[:, 0:128]
    for k in range(1, cols // 128):
        red = jnp.maximum(red, acc[:, 128 * k:128 * (k + 1)])
    return jnp.minimum(jnp.abs(red[0:1, :]), 0.0)


FF_BLOCK = 512
ROW_CHUNK = 128


def _widen(zero_row, width):
    return jnp.concatenate([zero_row] * (width // 128), axis=1)


def _swiglu_block(xn, wg_ref, wu_ref, act_scr, k, zero_row=None):
    lo = k * FF_BLOCK
    hi = min(lo + FF_BLOCK, wg_ref.shape[1])
    gate = _dot(xn, wg_ref[:, lo:hi])
    started = jnp.minimum(jnp.abs(gate[0:1, 0:128]), 0.0)
    if zero_row is not None:
        gate = gate + _widen(zero_row, hi - lo)
    act_scr[:, lo:hi] = (jax.nn.silu(gate) * _dot(xn, wu_ref[:, lo:hi])).astype(_BF16)
    return started


def _n_ff_blocks(wg_ref):
    return pl.cdiv(wg_ref.shape[1], FF_BLOCK)


def _row_chunks(n_rows):
    return [slice(r, r + ROW_CHUNK) for r in range(0, n_rows, ROW_CHUNK)]


def _ffn1_kernel(x_pre_ref, x_epi_ref, pre_g_ref, post_g_ref, next_g_ref, wg_ref, wu_ref, wd_ref,
                 h_ref, xn_next_ref, xn_scr, xn_new_scr, f_scr, act_scr):
    s = pl.program_id(0)
    last = pl.num_programs(0) - 1

    chunks = _row_chunks(x_pre_ref.shape[0])

    def prologue(rows):
        xn = _rms(x_pre_ref[rows, :], pre_g_ref[...])
        xn_new_scr[rows, :] = xn.astype(_BF16)
        return xn

    def epilogue(rows):
        h = x_epi_ref[rows, :] + 0.5 * _rms(f_scr[rows, :], post_g_ref[...])
        h_ref[rows, :] = h
        hn = _rms(h, next_g_ref[...])
        xn_next_ref[rows, :] = hn.astype(_BF16)
        return hn

    @pl.when(s == 0)
    def _():
        for rows in chunks:
            prologue(rows)
        xn_scr[...] = xn_new_scr[...]
        f_scr[...] = jnp.zeros(f_scr.shape, _F32)

    @pl.when((s > 0) & (s < last))
    def _():
        zero_rows = [None] + [_ordering_zero(epilogue(rows)) + _ordering_zero(prologue(rows)) for rows in chunks]
        zero_rows += [None] * (_n_ff_blocks(wg_ref) - len(zero_rows))
        xn = xn_scr[...]
        for k, zero_row in enumerate(zero_rows):
            _swiglu_block(xn, wg_ref, wu_ref, act_scr, k, zero_row)
        f_scr[...] = _dot(act_scr[...], wd_ref[...])
        xn_scr[...] = xn_new_scr[...]

    @pl.when(s == last)
    def _():
        for rows in chunks:
            epilogue(rows)


def _ffn2_ple_kernel(xn_ref, h_ref, p_ref, post_g_ref, wg_ref, wu_ref, wd_ref,
                     ple_pre_g_ref, ple_wg_ref, ple_wp_ref, ple_post_g_ref, o_ref, f_scr, act_scr):
    s = pl.program_id(0)
    last = pl.num_programs(0) - 1

    def epilogue(started=None):
        post_g = post_g_ref[...]
        if started is not None:
            post_g = post_g + _widen(started, post_g.shape[1])
        h = h_ref[...] + 0.5 * _rms(f_scr[...], post_g)
        gate = _dot(_rms(h, ple_pre_g_ref[...]).astype(_BF16), ple_wg_ref[...])
        e = _dot(p_ref[...].astype(_BF16), ple_wp_ref[...])
        outs = []
        for rows in _row_chunks(h.shape[0]):
            out = h[rows, :] + _rms(jax.nn.sigmoid(gate[rows, :]) * e[rows, :], ple_post_g_ref[...])
            o_ref[rows, :] = out
            outs.append(out)
        return outs

    @pl.when(s == 0)
    def _():
        f_scr[...] = jnp.zeros(f_scr.shape, _F32)

    @pl.when(s < last)
    def _():
        xn = xn_ref[...]
        started = _swiglu_block(xn, wg_ref, wu_ref, act_scr, 0)
        zero_rows = [_ordering_zero(out) for out in epilogue(started)]
        zero_rows = [None] * (_n_ff_blocks(wg_ref) - 1 - len(zero_rows)) + zero_rows
        for k, zero_row in enumerate(zero_rows, start=1):
            _swiglu_block(xn, wg_ref, wu_ref, act_scr, k, zero_row)
        f_scr[...] = _dot(act_scr[...], wd_ref[...])

    @pl.when(s == last)
    def _():
        epilogue()


def _mixer_kernel(xn_ref, h_ref, w_in_ref, sgu_g_ref, sgu_w_ref, sgu_bt_ref, pool_w_ref, pool_scale_ref,
                  w_out_a_ref, w_out_b_ref, w_o_ref, post_g_ref, next_g_ref,
                  h_out_ref, xn_next_ref, a_scr, b_scr, cext_scr, zu_scr, zv_scr, zga_scr, zgb_scr, m_scr,
                  *, tiles_per_seq):
    tm, d = xn_ref.shape
    dg = d // N_SGU_GROUPS
    s = pl.program_id(0)
    last = pl.num_programs(0) - 1

    def epilogue(rows):
        h = h_ref[rows, :] + _rms(m_scr[rows, :], post_g_ref[...])
        h_out_ref[rows, :] = h
        hn = _rms(h, next_g_ref[...])
        xn_next_ref[rows, :] = hn.astype(_BF16)
        return hn

    def mixer():
        j = lax.rem(s, tiles_per_seq)
        xn = xn_ref[...]
        chunks = _row_chunks(tm)

        @pl.when(j == 0)
        def _():
            cext_scr[0:HALO, :] = jnp.zeros((HALO, d), _F32)

        zv_scr[...] = _dot(xn, w_in_ref[:, d:2 * d])
        for rows in chunks:
            epilogue(rows)
        cext_scr[HALO:HALO + tm, :] = _dot(xn, w_in_ref[:, 2 * d:3 * d])
        zu_scr[...] = _dot(xn, w_in_ref[:, 0:d])

        v = jax.nn.gelu(zv_scr[...])
        mu = jnp.mean(v, axis=-1, keepdims=True)
        vc = v - mu
        v = (vc * lax.rsqrt(jnp.mean(vc * vc, axis=-1, keepdims=True) + EPS) * sgu_g_ref[...]).astype(_BF16)

        c = cext_scr[HALO:HALO + tm, :]
        t = j * tm + lax.broadcasted_iota(jnp.int32, (tm, 1), 0)
        dp = d // len(POOL_WINDOWS)
        diffs = []
        for gi, w in enumerate(POOL_WINDOWS):
            cs = slice(gi * dp, (gi + 1) * dp)
            acc = cext_scr[:, cs]
            span = 1
            while span < w:
                acc = acc + pltpu.roll(acc, span, axis=0)
                span *= 2
            count = jnp.minimum(t + 1, w).astype(_F32)
            diffs.append(acc[HALO:HALO + tm] / count - c[:, cs])
        cext_scr[0:HALO, :] = cext_scr[tm:tm + HALO, :]
        zgb_scr[...] = _dot(xn, w_in_ref[:, 4 * d:5 * d])

        u = jax.nn.gelu(zu_scr[...])
        zga_scr[...] = _dot(xn, w_in_ref[:, 3 * d:4 * d])

        row = lax.broadcasted_iota(jnp.int32, (CHUNK, CHUNK), 0)
        col = lax.broadcasted_iota(jnp.int32, (CHUNK, CHUNK), 1)
        for g in range(N_SGU_GROUPS):
            ws = jnp.where(row >= col, sgu_w_ref[g], 0.0).astype(_BF16)
            bias = sgu_bt_ref[:, g:g + 1]
            for ci in range(tm // CHUNK):
                rs = slice(ci * CHUNK, (ci + 1) * CHUNK)
                cs = slice(g * dg, (g + 1) * dg)
                sv = _dot(ws, v[rs, cs]) + bias
                a_scr[rs, cs] = (u[rs, cs] * sv).astype(_BF16)

        for gi, diff in enumerate(diffs):
            cs = slice(gi * dp, (gi + 1) * dp)
            b_scr[:, cs] = (_dot(diff.astype(_BF16), pool_w_ref[gi]) * pool_scale_ref[:, cs]).astype(_BF16)

        yb = jax.nn.sigmoid(zgb_scr[...]) * _dot(b_scr[...], w_out_b_ref[...])
        y = jax.nn.sigmoid(zga_scr[...]) * _dot(a_scr[...], w_out_a_ref[...]) + yb
        m_scr[...] = _dot(y.astype(_BF16), w_o_ref[...])

    @pl.when(s == 0)
    def _():
        m_scr[...] = jnp.zeros(m_scr.shape, _F32)

    @pl.when(s < last)
    def _():
        mixer()

    @pl.when(s == last)
    def _():
        for rows in _row_chunks(tm):
            epilogue(rows)


def _resident():
    return pl.BlockSpec(memory_space=pltpu.VMEM)


def _tile_spec(width, n_tiles, lag):
    return pl.BlockSpec((TOKEN_TILE, width), lambda s: (jnp.clip(s - lag, 0, n_tiles - 1), 0))


def _params():
    return pltpu.CompilerParams(dimension_semantics=("arbitrary",), vmem_limit_bytes=V7X_VMEM_LIMIT_BYTES)


def _ffn1_call(x, pre_g, post_g, next_g, wg, wu, wd):
    n, d = x.shape
    n_tiles = n // TOKEN_TILE
    return pl.pallas_call(
        _ffn1_kernel,
        grid=(n_tiles + 2,),
        in_specs=[_tile_spec(d, n_tiles, 0), _tile_spec(d, n_tiles, 2)] + [_resident()] * 6,
        out_specs=[_tile_spec(d, n_tiles, 2), _tile_spec(d, n_tiles, 2)],
        out_shape=[jax.ShapeDtypeStruct((n, d), _F32), jax.ShapeDtypeStruct((n, d), _BF16)],
        scratch_shapes=[
            pltpu.VMEM((TOKEN_TILE, d), _BF16),
            pltpu.VMEM((TOKEN_TILE, d), _BF16),
            pltpu.VMEM((TOKEN_TILE, d), _F32),
            pltpu.VMEM((TOKEN_TILE, wg.shape[1]), _BF16),
        ],
        compiler_params=_params(),
        name="ffn1",
    )(x, x, pre_g, post_g, next_g, wg, wu, wd)


def _mixer_call(xn, h, seq, w_in, sgu_g, sgu_w, sgu_bt, pool_w, pool_scale, w_out_a, w_out_b, w_o,
                post_g, next_g):
    n, d = h.shape
    n_tiles = n // TOKEN_TILE
    return pl.pallas_call(
        functools.partial(_mixer_kernel, tiles_per_seq=seq // TOKEN_TILE),
        grid=(n_tiles + 1,),
        in_specs=[_tile_spec(d, n_tiles, 0), _tile_spec(d, n_tiles, 1)] + [_resident()] * 11,
        out_specs=[_tile_spec(d, n_tiles, 1), _tile_spec(d, n_tiles, 1)],
        out_shape=[jax.ShapeDtypeStruct((n, d), _F32), jax.ShapeDtypeStruct((n, d), _BF16)],
        scratch_shapes=[
            pltpu.VMEM((TOKEN_TILE, d), _BF16),
            pltpu.VMEM((TOKEN_TILE, d), _BF16),
            pltpu.VMEM((TOKEN_TILE + HALO, d), _F32),
            pltpu.VMEM((TOKEN_TILE, d), _F32),
            pltpu.VMEM((TOKEN_TILE, d), _F32),
            pltpu.VMEM((TOKEN_TILE, d), _F32),
            pltpu.VMEM((TOKEN_TILE, d), _F32),
            pltpu.VMEM((TOKEN_TILE, d), _F32),
        ],
        compiler_params=_params(),
        name="mixer",
    )(xn, h, w_in, sgu_g, sgu_w, sgu_bt, pool_w, pool_scale, w_out_a, w_out_b, w_o, post_g, next_g)


def _ffn2_ple_call(xn, h, p, post_g, wg, wu, wd, ple_pre_g, ple_wg, ple_wp, ple_post_g):
    n, d = h.shape
    n_tiles = n // TOKEN_TILE
    return pl.pallas_call(
        _ffn2_ple_kernel,
        grid=(n_tiles + 1,),
        in_specs=[_tile_spec(d, n_tiles, 0), _tile_spec(d, n_tiles, 1), _tile_spec(p.shape[1], n_tiles, 1)]
        + [_resident()] * 8,
        out_specs=_tile_spec(d, n_tiles, 1),
        out_shape=jax.ShapeDtypeStruct((n, d), _F32),
        scratch_shapes=[
            pltpu.VMEM((TOKEN_TILE, d), _F32),
            pltpu.VMEM((TOKEN_TILE, wg.shape[1]), _BF16),
        ],
        compiler_params=_params(),
        name="ffn2_ple",
    )(xn, h, p, post_g, wg, wu, wd, ple_pre_g, ple_wg, ple_wp, ple_post_g)


def kernel(x, p, ffn1_pre_g, ffn1_w_gate, ffn1_w_up, ffn1_w_down, ffn1_post_g, mix_pre_g, w_in, sgu_norm_g, sgu_w, sgu_b, pool_w, pool_scale, w_out_a, w_out_b, w_o, mix_post_g, ffn2_pre_g, ffn2_w_gate, ffn2_w_up, ffn2_w_down, ffn2_post_g, ple_pre_g, ple_w_gate, ple_w_proj, ple_post_g):
    batch, seq, d = x.shape
    depth = p.shape[0]
    assert seq % TOKEN_TILE == 0 and TOKEN_TILE % CHUNK == 0
    bf = lambda w: w.astype(_BF16)
    row = lambda g: g.reshape(1, -1)

    h = x.reshape(batch * seq, d)
    for i in range(depth):
        h, xn = _ffn1_call(h, row(ffn1_pre_g[i]), row(ffn1_post_g[i]), row(mix_pre_g[i]),
                           bf(ffn1_w_gate[i]), bf(ffn1_w_up[i]), bf(ffn1_w_down[i]))
        h, xn = _mixer_call(xn, h, seq, bf(w_in[i]), row(sgu_norm_g[i]), sgu_w[i], sgu_b[i].T, bf(pool_w[i]),
                            row(pool_scale[i]), bf(w_out_a[i]), bf(w_out_b[i]), bf(w_o[i]),
                            row(mix_post_g[i]), row(ffn2_pre_g[i]))
        h = _ffn2_ple_call(xn, h, p[i].reshape(batch * seq, -1), row(ffn2_post_g[i]),
                           bf(ffn2_w_gate[i]), bf(ffn2_w_up[i]), bf(ffn2_w_down[i]),
                           row(ple_pre_g[i]), bf(ple_w_gate[i]), bf(ple_w_proj[i]), row(ple_post_g[i]))
    return h.reshape(batch, seq, d)
```

```python
import functools

import jax
import jax.numpy as jnp
from jax import lax
from jax.experimental import pallas as pl
from jax.experimental.pallas import tpu as pltpu

EPS = 1e-6
CHUNK = 128
N_SGU_GROUPS = 4
POOL_WINDOWS = (2, 4, 8, 16)
HALO = 16
TOKEN_TILE = 512
V7X_VMEM_LIMIT_BYTES = 58 * 1024 * 1024

_BF16 = jnp.bfloat16
_F32 = jnp.float32


def _rms(x, g):
    return x * lax.rsqrt(jnp.mean(x * x, axis=-1, keepdims=True) + EPS) * g


def _dot(a, b):
    return jnp.dot(a, b, preferred_element_type=_F32)


def _ordering_zero(x):
    rows, cols = x.shape
    acc = x[0:8, :]
    for r in range(1, rows // 8):
        acc = jnp.maximum(acc, x[8 * r:8 * r + 8, :])
    red = acc[:, 0:128]
    for k in range(1, cols // 128):
        red = jnp.maximum(red, acc[:, 128 * k:128 * (k + 1)])
    return jnp.minimum(jnp.abs(red[0:1, :]), 0.0)


FF_BLOCK = 512
ROW_CHUNK = 128


def _widen(zero_row, width):
    return jnp.concatenate([zero_row] * (width // 128), axis=1)


def _swiglu_block(xn, wg_ref, wu_ref, act_scr, k, zero_row=None):
    lo = k * FF_BLOCK
    hi = min(lo + FF_BLOCK, wg_ref.shape[1])
    gate = _dot(xn, wg_ref[:, lo:hi])
    started = jnp.minimum(jnp.abs(gate[0:1, 0:128]), 0.0)
    if zero_row is not None:
        gate = gate + _widen(zero_row, hi - lo)
    act_scr[:, lo:hi] = (jax.nn.silu(gate) * _dot(xn, wu_ref[:, lo:hi])).astype(_BF16)
    return started


def _activate_block(gate, up, act_scr, k, zero_row=None):
    lo = k * FF_BLOCK
    hi = min(lo + FF_BLOCK, gate.shape[1])
    g = gate[:, lo:hi]
    if zero_row is not None:
        g = g + _widen(zero_row, hi - lo)
    act_scr[:, lo:hi] = (jax.nn.silu(g) * up[:, lo:hi]).astype(_BF16)


def _n_ff_blocks(wg_ref):
    return pl.cdiv(wg_ref.shape[1], FF_BLOCK)


def _row_chunks(n_rows):
    return [slice(r, r + ROW_CHUNK) for r in range(0, n_rows, ROW_CHUNK)]


def _ffn1_kernel(x_pre_ref, x_epi_ref, pre_g_ref, post_g_ref, next_g_ref, wg_ref, wu_ref, wd_ref,
                 h_ref, xn_next_ref, xn_scr, xn_new_scr, f_scr, act_scr):
    s = pl.program_id(0)
    last = pl.num_programs(0) - 1

    chunks = _row_chunks(x_pre_ref.shape[0])

    def prologue(rows):
        xn = _rms(x_pre_ref[rows, :], pre_g_ref[...])
        xn_new_scr[rows, :] = xn.astype(_BF16)
        return xn

    def epilogue(rows):
        h = x_epi_ref[rows, :] + 0.5 * _rms(f_scr[rows, :], post_g_ref[...])
        h_ref[rows, :] = h
        hn = _rms(h, next_g_ref[...])
        xn_next_ref[rows, :] = hn.astype(_BF16)
        return hn

    @pl.when(s == 0)
    def _():
        for rows in chunks:
            prologue(rows)
        xn_scr[...] = xn_new_scr[...]
        f_scr[...] = jnp.zeros(f_scr.shape, _F32)

    @pl.when((s > 0) & (s < last))
    def _():
        zero_rows = [None] + [_ordering_zero(epilogue(rows)) + _ordering_zero(prologue(rows)) for rows in chunks]
        zero_rows += [None] * (_n_ff_blocks(wg_ref) - len(zero_rows))
        xn = xn_scr[...]
        gate = _dot(xn, wg_ref[...])
        up = _dot(xn, wu_ref[...])
        for k, zero_row in enumerate(zero_rows):
            _activate_block(gate, up, act_scr, k, zero_row)
        f_scr[...] = _dot(act_scr[...], wd_ref[...])
        xn_scr[...] = xn_new_scr[...]

    @pl.when(s == last)
    def _():
        for rows in chunks:
            epilogue(rows)


def _ffn2_ple_kernel(xn_ref, h_ref, p_ref, post_g_ref, wg_ref, wu_ref, wd_ref,
                     ple_pre_g_ref, ple_wg_ref, ple_wp_ref, ple_post_g_ref, o_ref, f_scr, act_scr):
    s = pl.program_id(0)
    last = pl.num_programs(0) - 1

    def epilogue(started=None):
        post_g = post_g_ref[...]
        if started is not None:
            post_g = post_g + _widen(started, post_g.shape[1])
        h = h_ref[...] + 0.5 * _rms(f_scr[...], post_g)
        gate = _dot(_rms(h, ple_pre_g_ref[...]).astype(_BF16), ple_wg_ref[...])
        e = _dot(p_ref[...].astype(_BF16), ple_wp_ref[...])
        outs = []
        for rows in _row_chunks(h.shape[0]):
            out = h[rows, :] + _rms(jax.nn.sigmoid(gate[rows, :]) * e[rows, :], ple_post_g_ref[...])
            o_ref[rows, :] = out
            outs.append(out)
        return outs

    @pl.when(s == 0)
    def _():
        f_scr[...] = jnp.zeros(f_scr.shape, _F32)

    @pl.when(s < last)
    def _():
        xn = xn_ref[...]
        started = _swiglu_block(xn, wg_ref, wu_ref, act_scr, 0)
        zero_rows = [_ordering_zero(out) for out in epilogue(started)]
        zero_rows = [None] * (_n_ff_blocks(wg_ref) - 1 - len(zero_rows)) + zero_rows
        for k, zero_row in enumerate(zero_rows, start=1):
            _swiglu_block(xn, wg_ref, wu_ref, act_scr, k, zero_row)
        f_scr[...] = _dot(act_scr[...], wd_ref[...])

    @pl.when(s == last)
    def _():
        epilogue()


def _mixer_kernel(xn_ref, h_ref, w_in_ref, sgu_g_ref, sgu_w_ref, sgu_bt_ref, pool_w_ref, pool_scale_ref,
                  w_out_a_ref, w_out_b_ref, w_o_ref, post_g_ref, next_g_ref,
                  h_out_ref, xn_next_ref, a_scr, b_scr, cext_scr, zu_scr, zv_scr, zga_scr, zgb_scr, m_scr,
                  *, tiles_per_seq):
    tm, d = xn_ref.shape
    dg = d // N_SGU_GROUPS
    s = pl.program_id(0)
    last = pl.num_programs(0) - 1

    def epilogue(rows):
        h = h_ref[rows, :] + _rms(m_scr[rows, :], post_g_ref[...])
        h_out_ref[rows, :] = h
        hn = _rms(h, next_g_ref[...])
        xn_next_ref[rows, :] = hn.astype(_BF16)
        return hn

    def mixer():
        j = lax.rem(s, tiles_per_seq)
        xn = xn_ref[...]
        chunks = _row_chunks(tm)

        @pl.when(j == 0)
        def _():
            cext_scr[0:HALO, :] = jnp.zeros((HALO, d), _F32)

        zv_scr[...] = _dot(xn, w_in_ref[:, d:2 * d])
        for rows in chunks:
            epilogue(rows)
        cext_scr[HALO:HALO + tm, :] = _dot(xn, w_in_ref[:, 2 * d:3 * d])
        zu_scr[...] = _dot(xn, w_in_ref[:, 0:d])

        v = jax.nn.gelu(zv_scr[...])
        mu = jnp.mean(v, axis=-1, keepdims=True)
        vc = v - mu
        v = (vc * lax.rsqrt(jnp.mean(vc * vc, axis=-1, keepdims=True) + EPS) * sgu_g_ref[...]).astype(_BF16)

        c = cext_scr[HALO:HALO + tm, :]
        t = j * tm + lax.broadcasted_iota(jnp.int32, (tm, 1), 0)
        dp = d // len(POOL_WINDOWS)
        diffs = []
        for gi, w in enumerate(POOL_WINDOWS):
            cs = slice(gi * dp, (gi + 1) * dp)
            acc = cext_scr[:, cs]
            span = 1
            while span < w:
                acc = acc + pltpu.roll(acc, span, axis=0)
                span *= 2
            count = jnp.minimum(t + 1, w).astype(_F32)
            diffs.append(acc[HALO:HALO + tm] / count - c[:, cs])
        cext_scr[0:HALO, :] = cext_scr[tm:tm + HALO, :]
        zgb_scr[...] = _dot(xn, w_in_ref[:, 4 * d:5 * d])

        u = jax.nn.gelu(zu_scr[...])
        zga_scr[...] = _dot(xn, w_in_ref[:, 3 * d:4 * d])

        row = lax.broadcasted_iota(jnp.int32, (CHUNK, CHUNK), 0)
        col = lax.broadcasted_iota(jnp.int32, (CHUNK, CHUNK), 1)
        for g in range(N_SGU_GROUPS):
            ws = jnp.where(row >= col, sgu_w_ref[g], 0.0).astype(_BF16)
            bias = sgu_bt_ref[:, g:g + 1]
            for ci in range(tm // CHUNK):
                rs = slice(ci * CHUNK, (ci + 1) * CHUNK)
                cs = slice(g * dg, (g + 1) * dg)
                sv = _dot(ws, v[rs, cs]) + bias
                a_scr[rs, cs] = (u[rs, cs] * sv).astype(_BF16)

        for gi, diff in enumerate(diffs):
            cs = slice(gi * dp, (gi + 1) * dp)
            b_scr[:, cs] = (_dot(diff.astype(_BF16), pool_w_ref[gi]) * pool_scale_ref[:, cs]).astype(_BF16)

        yb = jax.nn.sigmoid(zgb_scr[...]) * _dot(b_scr[...], w_out_b_ref[...])
        y = jax.nn.sigmoid(zga_scr[...]) * _dot(a_scr[...], w_out_a_ref[...]) + yb
        m_scr[...] = _dot(y.astype(_BF16), w_o_ref[...])

    @pl.when(s == 0)
    def _():
        m_scr[...] = jnp.zeros(m_scr.shape, _F32)

    @pl.when(s < last)
    def _():
        mixer()

    @pl.when(s == last)
    def _():
        for rows in _row_chunks(tm):
            epilogue(rows)


def _resident():
    return pl.BlockSpec(memory_space=pltpu.VMEM)


def _tile_spec(width, n_tiles, lag):
    return pl.BlockSpec((TOKEN_TILE, width), lambda s: (jnp.clip(s - lag, 0, n_tiles - 1), 0))


def _params():
    return pltpu.CompilerParams(dimension_semantics=("arbitrary",), vmem_limit_bytes=V7X_VMEM_LIMIT_BYTES)


def _ffn1_call(x, pre_g, post_g, next_g, wg, wu, wd):
    n, d = x.shape
    n_tiles = n // TOKEN_TILE
    return pl.pallas_call(
        _ffn1_kernel,
        grid=(n_tiles + 2,),
        in_specs=[_tile_spec(d, n_tiles, 0), _tile_spec(d, n_tiles, 2)] + [_resident()] * 6,
        out_specs=[_tile_spec(d, n_tiles, 2), _tile_spec(d, n_tiles, 2)],
        out_shape=[jax.ShapeDtypeStruct((n, d), _F32), jax.ShapeDtypeStruct((n, d), _BF16)],
        scratch_shapes=[
            pltpu.VMEM((TOKEN_TILE, d), _BF16),
            pltpu.VMEM((TOKEN_TILE, d), _BF16),
            pltpu.VMEM((TOKEN_TILE, d), _F32),
            pltpu.VMEM((TOKEN_TILE, wg.shape[1]), _BF16),
        ],
        compiler_params=_params(),
        name="ffn1",
    )(x, x, pre_g, post_g, next_g, wg, wu, wd)


def _mixer_call(xn, h, seq, w_in, sgu_g, sgu_w, sgu_bt, pool_w, pool_scale, w_out_a, w_out_b, w_o,
                post_g, next_g):
    n, d = h.shape
    n_tiles = n // TOKEN_TILE
    return pl.pallas_call(
        functools.partial(_mixer_kernel, tiles_per_seq=seq // TOKEN_TILE),
        grid=(n_tiles + 1,),
        in_specs=[_tile_spec(d, n_tiles, 0), _tile_spec(d, n_tiles, 1)] + [_resident()] * 11,
        out_specs=[_tile_spec(d, n_tiles, 1), _tile_spec(d, n_tiles, 1)],
        out_shape=[jax.ShapeDtypeStruct((n, d), _F32), jax.ShapeDtypeStruct((n, d), _BF16)],
        scratch_shapes=[
            pltpu.VMEM((TOKEN_TILE, d), _BF16),
            pltpu.VMEM((TOKEN_TILE, d), _BF16),
            pltpu.VMEM((TOKEN_TILE + HALO, d), _F32),
            pltpu.VMEM((TOKEN_TILE, d), _F32),
            pltpu.VMEM((TOKEN_TILE, d), _F32),
            pltpu.VMEM((TOKEN_TILE, d), _F32),
            pltpu.VMEM((TOKEN_TILE, d), _F32),
            pltpu.VMEM((TOKEN_TILE, d), _F32),
        ],
        compiler_params=_params(),
        name="mixer",
    )(xn, h, w_in, sgu_g, sgu_w, sgu_bt, pool_w, pool_scale, w_out_a, w_out_b, w_o, post_g, next_g)


def _ffn2_ple_call(xn, h, p, post_g, wg, wu, wd, ple_pre_g, ple_wg, ple_wp, ple_post_g):
    n, d = h.shape
    n_tiles = n // TOKEN_TILE
    return pl.pallas_call(
        _ffn2_ple_kernel,
        grid=(n_tiles + 1,),
        in_specs=[_tile_spec(d, n_tiles, 0), _tile_spec(d, n_tiles, 1), _tile_spec(p.shape[1], n_tiles, 1)]
        + [_resident()] * 8,
        out_specs=_tile_spec(d, n_tiles, 1),
        out_shape=jax.ShapeDtypeStruct((n, d), _F32),
        scratch_shapes=[
            pltpu.VMEM((TOKEN_TILE, d), _F32),
            pltpu.VMEM((TOKEN_TILE, wg.shape[1]), _BF16),
        ],
        compiler_params=_params(),
        name="ffn2_ple",
    )(xn, h, p, post_g, wg, wu, wd, ple_pre_g, ple_wg, ple_wp, ple_post_g)


def kernel(x, p, ffn1_pre_g, ffn1_w_gate, ffn1_w_up, ffn1_w_down, ffn1_post_g, mix_pre_g, w_in, sgu_norm_g, sgu_w, sgu_b, pool_w, pool_scale, w_out_a, w_out_b, w_o, mix_post_g, ffn2_pre_g, ffn2_w_gate, ffn2_w_up, ffn2_w_down, ffn2_post_g, ple_pre_g, ple_w_gate, ple_w_proj, ple_post_g):
    batch, seq, d = x.shape
    depth = p.shape[0]
    assert seq % TOKEN_TILE == 0 and TOKEN_TILE % CHUNK == 0
    bf = lambda w: w.astype(_BF16)
    row = lambda g: g.reshape(1, -1)

    h = x.reshape(batch * seq, d)
    for i in range(depth):
        h, xn = _ffn1_call(h, row(ffn1_pre_g[i]), row(ffn1_post_g[i]), row(mix_pre_g[i]),
                           bf(ffn1_w_gate[i]), bf(ffn1_w_up[i]), bf(ffn1_w_down[i]))
        h, xn = _mixer_call(xn, h, seq, bf(w_in[i]), row(sgu_norm_g[i]), sgu_w[i], sgu_b[i].T, bf(pool_w[i]),
                            row(pool_scale[i]), bf(w_out_a[i]), bf(w_out_b[i]), bf(w_o[i]),
                            row(mix_post_g[i]), row(ffn2_pre_g[i]))
        h = _ffn2_ple_call(xn, h, p[i].reshape(batch * seq, -1), row(ffn2_post_g[i]),
                           bf(ffn2_w_gate[i]), bf(ffn2_w_up[i]), bf(ffn2_w_down[i]),
                           row(ple_pre_g[i]), bf(ple_w_gate[i]), bf(ple_w_proj[i]), row(ple_post_g[i]))
    return h.reshape(batch, seq, d)
```

```python
import functools

import jax
import jax.numpy as jnp
from jax import lax
from jax.experimental import pallas as pl
from jax.experimental.pallas import tpu as pltpu

EPS = 1e-6
CHUNK = 128
N_SGU_GROUPS = 4
POOL_WINDOWS = (2, 4, 8, 16)
HALO = 16
TOKEN_TILE = 512
V7X_VMEM_LIMIT_BYTES = 58 * 1024 * 1024
FF_BLOCK = 256
ROW_CHUNK = 128

_BF16 = jnp.bfloat16
_F32 = jnp.float32


def _rms(x, g):
    return x * lax.rsqrt(jnp.mean(x * x, axis=-1, keepdims=True) + EPS) * g


def _dot(a, b):
    return jnp.dot(a, b, preferred_element_type=_F32)


def _ordering_zero(x):
    rows, cols = x.shape
    acc = x[0:8, :]
    for r in range(1, rows // 8):
        acc = jnp.maximum(acc, x[8 * r:8 * r + 8, :])
    red = acc[:, 0:128]
    for k in range(1, cols // 128):
        red = jnp.maximum(red, acc[:, 128 * k:128 * (k + 1)])
    return jnp.minimum(jnp.abs(red[0:1, :]), 0.0)


def _widen(zero_row, width):
    return jnp.concatenate([zero_row] * (width // 128), axis=1)


def _interleave_gate_up(w_gate, w_up):
    d, f = w_gate.shape
    blocks = jnp.stack([w_gate.reshape(d, f // FF_BLOCK, FF_BLOCK), w_up.reshape(d, f // FF_BLOCK, FF_BLOCK)], axis=2)
    return blocks.reshape(d, 2 * f)


def _swiglu_block(xn, wgu_ref, act_scr, k, zero_row=None):
    gu = _dot(xn, wgu_ref[:, 2 * k * FF_BLOCK:2 * (k + 1) * FF_BLOCK])
    gate = gu[:, 0:FF_BLOCK]
    started = jnp.minimum(jnp.abs(gate[0:1, 0:128]), 0.0)
    if zero_row is not None:
        gate = gate + _widen(zero_row, FF_BLOCK)
    act_scr[:, k * FF_BLOCK:(k + 1) * FF_BLOCK] = (jax.nn.silu(gate) * gu[:, FF_BLOCK:2 * FF_BLOCK]).astype(_BF16)
    return started


def _n_ff_blocks(wgu_ref):
    return wgu_ref.shape[1] // (2 * FF_BLOCK)


def _row_chunks(n_rows):
    return [slice(r, r + ROW_CHUNK) for r in range(0, n_rows, ROW_CHUNK)]


def _ffn1_kernel(x_pre_ref, x_epi_ref, pre_g_ref, post_g_ref, next_g_ref, wgu_ref, wd_ref,
                 h_ref, xn_next_ref, xn_scr, xn_new_scr, f_scr, act_scr):
    s = pl.program_id(0)
    last = pl.num_programs(0) - 1

    chunks = _row_chunks(x_pre_ref.shape[0])

    def prologue(rows):
        xn = _rms(x_pre_ref[rows, :], pre_g_ref[...])
        xn_new_scr[rows, :] = xn.astype(_BF16)
        return xn

    def epilogue(rows):
        h = x_epi_ref[rows, :] + 0.5 * _rms(f_scr[rows, :], post_g_ref[...])
        h_ref[rows, :] = h
        hn = _rms(h, next_g_ref[...])
        xn_next_ref[rows, :] = hn.astype(_BF16)
        return hn

    @pl.when(s == 0)
    def _():
        for rows in chunks:
            prologue(rows)
        xn_scr[...] = xn_new_scr[...]
        f_scr[...] = jnp.zeros(f_scr.shape, _F32)

    @pl.when((s > 0) & (s < last))
    def _():
        zero_rows = [None]
        for rows in chunks:
            zero_rows += [_ordering_zero(epilogue(rows)), _ordering_zero(prologue(rows))]
        zero_rows += [None] * (_n_ff_blocks(wgu_ref) - len(zero_rows))
        xn = xn_scr[...]
        for k, zero_row in enumerate(zero_rows):
            _swiglu_block(xn, wgu_ref, act_scr, k, zero_row)
        f_scr[...] = _dot(act_scr[...], wd_ref[...])
        xn_scr[...] = xn_new_scr[...]

    @pl.when(s == last)
    def _():
        for rows in chunks:
            epilogue(rows)


def _ffn2_ple_kernel(xn_ref, h_ref, p_ref, post_g_ref, wgu_ref, wd_ref,
                     ple_pre_g_ref, ple_wg_ref, ple_wp_ref, ple_post_g_ref, o_ref,
                     f_scr, act_scr, h_scr, xn_ple_scr):
    s = pl.program_id(0)
    last = pl.num_programs(0) - 1

    chunks = _row_chunks(h_ref.shape[0])

    def residual(rows, started=None):
        post_g = post_g_ref[...]
        if started is not None:
            post_g = post_g + _widen(started, post_g.shape[1])
        h = h_ref[rows, :] + 0.5 * _rms(f_scr[rows, :], post_g)
        h_scr[rows, :] = h
        xn_ple_scr[rows, :] = _rms(h, ple_pre_g_ref[...]).astype(_BF16)

    def embed(rows, gate, e):
        out = h_scr[rows, :] + _rms(jax.nn.sigmoid(gate[rows, :]) * e[rows, :], ple_post_g_ref[...])
        o_ref[rows, :] = out
        return out

    def ple_matmuls():
        return _dot(xn_ple_scr[...], ple_wg_ref[...]), _dot(p_ref[...].astype(_BF16), ple_wp_ref[...])

    @pl.when(s == 0)
    def _():
        f_scr[...] = jnp.zeros(f_scr.shape, _F32)

    @pl.when(s < last)
    def _():
        xn = xn_ref[...]
        n_blocks = _n_ff_blocks(wgu_ref)
        for k, rows in enumerate(chunks):
            residual(rows, _swiglu_block(xn, wgu_ref, act_scr, k))
        gate, e = ple_matmuls()
        zero_rows = [None] * (n_blocks - 2 * len(chunks))
        for rows in chunks:
            zero_rows.append(_ordering_zero(embed(rows, gate, e)))
        for k, zero_row in enumerate(zero_rows, start=len(chunks)):
            _swiglu_block(xn, wgu_ref, act_scr, k, zero_row)
        f_scr[...] = _dot(act_scr[...], wd_ref[...])

    @pl.when(s == last)
    def _():
        for rows in chunks:
            residual(rows)
        gate, e = ple_matmuls()
        for rows in chunks:
            embed(rows, gate, e)


def _mixer_kernel(xn_ref, h_ref, w_in_ref, sgu_g_ref, sgu_w_ref, sgu_bt_ref, pool_w_ref, pool_scale_ref,
                  w_out_a_ref, w_out_b_ref, w_o_ref, post_g_ref, next_g_ref,
                  h_out_ref, xn_next_ref, a_scr, b_scr, cext_scr, zu_scr, zv_scr, zga_scr, zgb_scr, m_scr,
                  *, tiles_per_seq):
    tm, d = xn_ref.shape
    dg = d // N_SGU_GROUPS
    s = pl.program_id(0)
    last = pl.num_programs(0) - 1

    def epilogue(rows):
        h = h_ref[rows, :] + _rms(m_scr[rows, :], post_g_ref[...])
        h_out_ref[rows, :] = h
        hn = _rms(h, next_g_ref[...])
        xn_next_ref[rows, :] = hn.astype(_BF16)
        return hn

    def mixer():
        j = lax.rem(s, tiles_per_seq)
        xn = xn_ref[...]
        chunks = _row_chunks(tm)

        @pl.when(j == 0)
        def _():
            cext_scr[0:HALO, :] = jnp.zeros((HALO, d), _F32)

        zv_scr[...] = _dot(xn, w_in_ref[:, d:2 * d])
        for rows in chunks:
            epilogue(rows)
        cext_scr[HALO:HALO + tm, :] = _dot(xn, w_in_ref[:, 2 * d:3 * d])
        zu_scr[...] = _dot(xn, w_in_ref[:, 0:d])

        v = jax.nn.gelu(zv_scr[...])
        mu = jnp.mean(v, axis=-1, keepdims=True)
        vc = v - mu
        v = (vc * lax.rsqrt(jnp.mean(vc * vc, axis=-1, keepdims=True) + EPS) * sgu_g_ref[...]).astype(_BF16)

        c = cext_scr[HALO:HALO + tm, :]
        t = j * tm + lax.broadcasted_iota(jnp.int32, (tm, 1), 0)
        dp = d // len(POOL_WINDOWS)
        diffs = []
        for gi, w in enumerate(POOL_WINDOWS):
            cs = slice(gi * dp, (gi + 1) * dp)
            acc = cext_scr[:, cs]
            span = 1
            while span < w:
                acc = acc + pltpu.roll(acc, span, axis=0)
                span *= 2
            count = jnp.minimum(t + 1, w).astype(_F32)
            diffs.append(acc[HALO:HALO + tm] / count - c[:, cs])
        cext_scr[0:HALO, :] = cext_scr[tm:tm + HALO, :]
        zgb_scr[...] = _dot(xn, w_in_ref[:, 4 * d:5 * d])

        u = jax.nn.gelu(zu_scr[...])
        zga_scr[...] = _dot(xn, w_in_ref[:, 3 * d:4 * d])

        row = lax.broadcasted_iota(jnp.int32, (CHUNK, CHUNK), 0)
        col = lax.broadcasted_iota(jnp.int32, (CHUNK, CHUNK), 1)
        for g in range(N_SGU_GROUPS):
            ws = jnp.where(row >= col, sgu_w_ref[g], 0.0).astype(_BF16)
            bias = sgu_bt_ref[:, g:g + 1]
            for ci in range(tm // CHUNK):
                rs = slice(ci * CHUNK, (ci + 1) * CHUNK)
                cs = slice(g * dg, (g + 1) * dg)
                sv = _dot(ws, v[rs, cs]) + bias
                a_scr[rs, cs] = (u[rs, cs] * sv).astype(_BF16)

        for gi, diff in enumerate(diffs):
            cs = slice(gi * dp, (gi + 1) * dp)
            b_scr[:, cs] = (_dot(diff.astype(_BF16), pool_w_ref[gi]) * pool_scale_ref[:, cs]).astype(_BF16)

        yb = jax.nn.sigmoid(zgb_scr[...]) * _dot(b_scr[...], w_out_b_ref[...])
        y = jax.nn.sigmoid(zga_scr[...]) * _dot(a_scr[...], w_out_a_ref[...]) + yb
        m_scr[...] = _dot(y.astype(_BF16), w_o_ref[...])

    @pl.when(s == 0)
    def _():
        m_scr[...] = jnp.zeros(m_scr.shape, _F32)

    @pl.when(s < last)
    def _():
        mixer()

    @pl.when(s == last)
    def _():
        for rows in _row_chunks(tm):
            epilogue(rows)


def _resident():
    return pl.BlockSpec(memory_space=pltpu.VMEM)


def _tile_spec(width, n_tiles, lag):
    return pl.BlockSpec((TOKEN_TILE, width), lambda s: (jnp.clip(s - lag, 0, n_tiles - 1), 0))


def _params():
    return pltpu.CompilerParams(dimension_semantics=("arbitrary",), vmem_limit_bytes=V7X_VMEM_LIMIT_BYTES)


def _ffn1_call(x, pre_g, post_g, next_g, wgu, wd):
    n, d = x.shape
    n_tiles = n // TOKEN_TILE
    return pl.pallas_call(
        _ffn1_kernel,
        grid=(n_tiles + 2,),
        in_specs=[_tile_spec(d, n_tiles, 0), _tile_spec(d, n_tiles, 2)] + [_resident()] * 5,
        out_specs=[_tile_spec(d, n_tiles, 2), _tile_spec(d, n_tiles, 2)],
        out_shape=[jax.ShapeDtypeStruct((n, d), _F32), jax.ShapeDtypeStruct((n, d), _BF16)],
        scratch_shapes=[
            pltpu.VMEM((TOKEN_TILE, d), _BF16),
            pltpu.VMEM((TOKEN_TILE, d), _BF16),
            pltpu.VMEM((TOKEN_TILE, d), _F32),
            pltpu.VMEM((TOKEN_TILE, wd.shape[0]), _BF16),
        ],
        compiler_params=_params(),
        name="ffn1",
    )(x, x, pre_g, post_g, next_g, wgu, wd)


def _mixer_call(xn, h, seq, w_in, sgu_g, sgu_w, sgu_bt, pool_w, pool_scale, w_out_a, w_out_b, w_o,
                post_g, next_g):
    n, d = h.shape
    n_tiles = n // TOKEN_TILE
    return pl.pallas_call(
        functools.partial(_mixer_kernel, tiles_per_seq=seq // TOKEN_TILE),
        grid=(n_tiles + 1,),
        in_specs=[_tile_spec(d, n_tiles, 0), _tile_spec(d, n_tiles, 1)] + [_resident()] * 11,
        out_specs=[_tile_spec(d, n_tiles, 1), _tile_spec(d, n_tiles, 1)],
        out_shape=[jax.ShapeDtypeStruct((n, d), _F32), jax.ShapeDtypeStruct((n, d), _BF16)],
        scratch_shapes=[
            pltpu.VMEM((TOKEN_TILE, d), _BF16),
            pltpu.VMEM((TOKEN_TILE, d), _BF16),
            pltpu.VMEM((TOKEN_TILE + HALO, d), _F32),
            pltpu.VMEM((TOKEN_TILE, d), _F32),
            pltpu.VMEM((TOKEN_TILE, d), _F32),
            pltpu.VMEM((TOKEN_TILE, d), _F32),
            pltpu.VMEM((TOKEN_TILE, d), _F32),
            pltpu.VMEM((TOKEN_TILE, d), _F32),
        ],
        compiler_params=_params(),
        name="mixer",
    )(xn, h, w_in, sgu_g, sgu_w, sgu_bt, pool_w, pool_scale, w_out_a, w_out_b, w_o, post_g, next_g)


def _ffn2_ple_call(xn, h, p, post_g, wgu, wd, ple_pre_g, ple_wg, ple_wp, ple_post_g):
    n, d = h.shape
    n_tiles = n // TOKEN_TILE
    return pl.pallas_call(
        _ffn2_ple_kernel,
        grid=(n_tiles + 1,),
        in_specs=[_tile_spec(d, n_tiles, 0), _tile_spec(d, n_tiles, 1), _tile_spec(p.shape[1], n_tiles, 1)]
        + [_resident()] * 7,
        out_specs=_tile_spec(d, n_tiles, 1),
        out_shape=jax.ShapeDtypeStruct((n, d), _F32),
        scratch_shapes=[
            pltpu.VMEM((TOKEN_TILE, d), _F32),
            pltpu.VMEM((TOKEN_TILE, wd.shape[0]), _BF16),
            pltpu.VMEM((TOKEN_TILE, d), _F32),
            pltpu.VMEM((TOKEN_TILE, d), _BF16),
        ],
        compiler_params=_params(),
        name="ffn2_ple",
    )(xn, h, p, post_g, wgu, wd, ple_pre_g, ple_wg, ple_wp, ple_post_g)


def kernel(x, p, ffn1_pre_g, ffn1_w_gate, ffn1_w_up, ffn1_w_down, ffn1_post_g, mix_pre_g, w_in, sgu_norm_g, sgu_w, sgu_b, pool_w, pool_scale, w_out_a, w_out_b, w_o, mix_post_g, ffn2_pre_g, ffn2_w_gate, ffn2_w_up, ffn2_w_down, ffn2_post_g, ple_pre_g, ple_w_gate, ple_w_proj, ple_post_g):
    batch, seq, d = x.shape
    depth = p.shape[0]
    assert seq % TOKEN_TILE == 0 and TOKEN_TILE % CHUNK == 0
    bf = lambda w: w.astype(_BF16)
    row = lambda g: g.reshape(1, -1)

    h = x.reshape(batch * seq, d)
    for i in range(depth):
        h, xn = _ffn1_call(h, row(ffn1_pre_g[i]), row(ffn1_post_g[i]), row(mix_pre_g[i]),
                           _interleave_gate_up(bf(ffn1_w_gate[i]), bf(ffn1_w_up[i])), bf(ffn1_w_down[i]))
        h, xn = _mixer_call(xn, h, seq, bf(w_in[i]), row(sgu_norm_g[i]), sgu_w[i], sgu_b[i].T, bf(pool_w[i]),
                            row(pool_scale[i]), bf(w_out_a[i]), bf(w_out_b[i]), bf(w_o[i]),
                            row(mix_post_g[i]), row(ffn2_pre_g[i]))
        h = _ffn2_ple_call(xn, h, p[i].reshape(batch * seq, -1), row(ffn2_post_g[i]),
                           _interleave_gate_up(bf(ffn2_w_gate[i]), bf(ffn2_w_up[i])), bf(ffn2_w_down[i]),
                           row(ple_pre_g[i]), bf(ple_w_gate[i]), bf(ple_w_proj[i]), row(ple_post_g[i]))
    return h.reshape(batch, seq, d)
```

```python
import functools

import jax
import jax.numpy as jnp
from jax import lax
from jax.experimental import pallas as pl
from jax.experimental.pallas import tpu as pltpu

EPS = 1e-6
CHUNK = 128
N_SGU_GROUPS = 4
POOL_WINDOWS = (2, 4, 8, 16)
HALO = 16
TOKEN_TILE = 512
V7X_VMEM_LIMIT_BYTES = 58 * 1024 * 1024
FF_BLOCK = 256
ROW_CHUNK = 128

_BF16 = jnp.bfloat16
_F32 = jnp.float32


def _rms(x, g):
    return x * lax.rsqrt(jnp.mean(x * x, axis=-1, keepdims=True) + EPS) * g


def _dot(a, b):
    return jnp.dot(a, b, preferred_element_type=_F32)


def _ordering_zero(x):
    rows, cols = x.shape
    acc = x[0:8, :]
    for r in range(1, rows // 8):
        acc = jnp.maximum(acc, x[8 * r:8 * r + 8, :])
    red = acc[:, 0:128]
    for k in range(1, cols // 128):
        red = jnp.maximum(red, acc[:, 128 * k:128 * (k + 1)])
    return jnp.minimum(jnp.abs(red[0:1, :]), 0.0)


def _widen(zero_row, width):
    return jnp.concatenate([zero_row] * (width // 128), axis=1)


def _swiglu_block(xn, wg_ref, wu_ref, act_scr, k, zero_row=None):
    cols = slice(k * FF_BLOCK, (k + 1) * FF_BLOCK)
    gate = _dot(xn, wg_ref[:, cols])
    up = _dot(xn, wu_ref[:, cols])
    started = jnp.minimum(jnp.abs(gate[0:1, 0:128]), 0.0)
    if zero_row is not None:
        gate = gate + _widen(zero_row, FF_BLOCK)
    act_scr[:, cols] = (jax.nn.silu(gate) * up).astype(_BF16)
    return started


def _n_ff_blocks(wg_ref):
    return wg_ref.shape[1] // FF_BLOCK


def _row_chunks(n_rows):
    return [slice(r, r + ROW_CHUNK) for r in range(0, n_rows, ROW_CHUNK)]


def _ffn1_kernel(x_pre_ref, x_epi_ref, pre_g_ref, post_g_ref, next_g_ref, wg_ref, wu_ref, wd_ref,
                 h_ref, xn_next_ref, xn_scr, xn_new_scr, f_scr, act_scr):
    s = pl.program_id(0)
    last = pl.num_programs(0) - 1

    chunks = _row_chunks(x_pre_ref.shape[0])

    def prologue(rows):
        xn = _rms(x_pre_ref[rows, :], pre_g_ref[...])
        xn_new_scr[rows, :] = xn.astype(_BF16)
        return xn

    def epilogue(rows):
        h = x_epi_ref[rows, :] + 0.5 * _rms(f_scr[rows, :], post_g_ref[...])
        h_ref[rows, :] = h
        hn = _rms(h, next_g_ref[...])
        xn_next_ref[rows, :] = hn.astype(_BF16)
        return hn

    @pl.when(s == 0)
    def _():
        for rows in chunks:
            prologue(rows)
        xn_scr[...] = xn_new_scr[...]
        f_scr[...] = jnp.zeros(f_scr.shape, _F32)

    @pl.when((s > 0) & (s < last))
    def _():
        zero_rows = [None]
        for rows in chunks:
            zero_rows += [_ordering_zero(epilogue(rows)), _ordering_zero(prologue(rows))]
        zero_rows += [None] * (_n_ff_blocks(wg_ref) - len(zero_rows))
        xn = xn_scr[...]
        for k, zero_row in enumerate(zero_rows):
            _swiglu_block(xn, wg_ref, wu_ref, act_scr, k, zero_row)
        f_scr[...] = _dot(act_scr[...], wd_ref[...])
        xn_scr[...] = xn_new_scr[...]

    @pl.when(s == last)
    def _():
        for rows in chunks:
            epilogue(rows)


def _ffn2_ple_kernel(xn_ref, h_ref, p_ref, post_g_ref, wg_ref, wu_ref, wd_ref,
                     ple_pre_g_ref, ple_wg_ref, ple_wp_ref, ple_post_g_ref, o_ref,
                     f_scr, act_scr, h_scr, xn_ple_scr):
    s = pl.program_id(0)
    last = pl.num_programs(0) - 1

    chunks = _row_chunks(h_ref.shape[0])

    def residual(rows, started=None):
        post_g = post_g_ref[...]
        if started is not None:
            post_g = post_g + _widen(started, post_g.shape[1])
        h = h_ref[rows, :] + 0.5 * _rms(f_scr[rows, :], post_g)
        h_scr[rows, :] = h
        xn_ple_scr[rows, :] = _rms(h, ple_pre_g_ref[...]).astype(_BF16)

    def embed(rows, gate, e):
        out = h_scr[rows, :] + _rms(jax.nn.sigmoid(gate[rows, :]) * e[rows, :], ple_post_g_ref[...])
        o_ref[rows, :] = out
        return out

    def ple_matmuls():
        return _dot(xn_ple_scr[...], ple_wg_ref[...]), _dot(p_ref[...].astype(_BF16), ple_wp_ref[...])

    @pl.when(s == 0)
    def _():
        f_scr[...] = jnp.zeros(f_scr.shape, _F32)

    @pl.when(s < last)
    def _():
        xn = xn_ref[...]
        n_blocks = _n_ff_blocks(wg_ref)
        for k, rows in enumerate(chunks):
            residual(rows, _swiglu_block(xn, wg_ref, wu_ref, act_scr, k))
        gate, e = ple_matmuls()
        zero_rows = [None] * (n_blocks - 2 * len(chunks))
        for rows in chunks:
            zero_rows.append(_ordering_zero(embed(rows, gate, e)))
        for k, zero_row in enumerate(zero_rows, start=len(chunks)):
            _swiglu_block(xn, wg_ref, wu_ref, act_scr, k, zero_row)
        f_scr[...] = _dot(act_scr[...], wd_ref[...])

    @pl.when(s == last)
    def _():
        for rows in chunks:
            residual(rows)
        gate, e = ple_matmuls()
        for rows in chunks:
            embed(rows, gate, e)


def _mixer_kernel(xn_ref, h_ref, w_in_ref, sgu_g_ref, sgu_w_ref, sgu_bt_ref, pool_w_ref, pool_scale_ref,
                  w_out_a_ref, w_out_b_ref, w_o_ref, post_g_ref, next_g_ref,
                  h_out_ref, xn_next_ref, a_scr, b_scr, cext_scr, zu_scr, zv_scr, zga_scr, zgb_scr, m_scr,
                  *, tiles_per_seq):
    tm, d = xn_ref.shape
    dg = d // N_SGU_GROUPS
    s = pl.program_id(0)
    last = pl.num_programs(0) - 1

    def epilogue(rows):
        h = h_ref[rows, :] + _rms(m_scr[rows, :], post_g_ref[...])
        h_out_ref[rows, :] = h
        hn = _rms(h, next_g_ref[...])
        xn_next_ref[rows, :] = hn.astype(_BF16)
        return hn

    def mixer():
        j = lax.rem(s, tiles_per_seq)
        xn = xn_ref[...]
        chunks = _row_chunks(tm)

        @pl.when(j == 0)
        def _():
            cext_scr[0:HALO, :] = jnp.zeros((HALO, d), _F32)

        zv_scr[...] = _dot(xn, w_in_ref[:, d:2 * d])
        for rows in chunks:
            epilogue(rows)
        cext_scr[HALO:HALO + tm, :] = _dot(xn, w_in_ref[:, 2 * d:3 * d])
        zu_scr[...] = _dot(xn, w_in_ref[:, 0:d])

        v = jax.nn.gelu(zv_scr[...])
        mu = jnp.mean(v, axis=-1, keepdims=True)
        vc = v - mu
        v = (vc * lax.rsqrt(jnp.mean(vc * vc, axis=-1, keepdims=True) + EPS) * sgu_g_ref[...]).astype(_BF16)

        c = cext_scr[HALO:HALO + tm, :]
        t = j * tm + lax.broadcasted_iota(jnp.int32, (tm, 1), 0)
        dp = d // len(POOL_WINDOWS)
        diffs = []
        for gi, w in enumerate(POOL_WINDOWS):
            cs = slice(gi * dp, (gi + 1) * dp)
            acc = cext_scr[:, cs]
            span = 1
            while span < w:
                acc = acc + pltpu.roll(acc, span, axis=0)
                span *= 2
            count = jnp.minimum(t + 1, w).astype(_F32)
            diffs.append(acc[HALO:HALO + tm] / count - c[:, cs])
        cext_scr[0:HALO, :] = cext_scr[tm:tm + HALO, :]
        zgb_scr[...] = _dot(xn, w_in_ref[:, 4 * d:5 * d])

        u = jax.nn.gelu(zu_scr[...])
        zga_scr[...] = _dot(xn, w_in_ref[:, 3 * d:4 * d])

        row = lax.broadcasted_iota(jnp.int32, (CHUNK, CHUNK), 0)
        col = lax.broadcasted_iota(jnp.int32, (CHUNK, CHUNK), 1)
        for g in range(N_SGU_GROUPS):
            ws = jnp.where(row >= col, sgu_w_ref[g], 0.0).astype(_BF16)
            bias = sgu_bt_ref[:, g:g + 1]
            for ci in range(tm // CHUNK):
                rs = slice(ci * CHUNK, (ci + 1) * CHUNK)
                cs = slice(g * dg, (g + 1) * dg)
                sv = _dot(ws, v[rs, cs]) + bias
                a_scr[rs, cs] = (u[rs, cs] * sv).astype(_BF16)

        for gi, diff in enumerate(diffs):
            cs = slice(gi * dp, (gi + 1) * dp)
            b_scr[:, cs] = (_dot(diff.astype(_BF16), pool_w_ref[gi]) * pool_scale_ref[:, cs]).astype(_BF16)

        yb = jax.nn.sigmoid(zgb_scr[...]) * _dot(b_scr[...], w_out_b_ref[...])
        y = jax.nn.sigmoid(zga_scr[...]) * _dot(a_scr[...], w_out_a_ref[...]) + yb
        m_scr[...] = _dot(y.astype(_BF16), w_o_ref[...])

    @pl.when(s == 0)
    def _():
        m_scr[...] = jnp.zeros(m_scr.shape, _F32)

    @pl.when(s < last)
    def _():
        mixer()

    @pl.when(s == last)
    def _():
        for rows in _row_chunks(tm):
            epilogue(rows)


def _resident():
    return pl.BlockSpec(memory_space=pltpu.VMEM)


def _tile_spec(width, n_tiles, lag):
    return pl.BlockSpec((TOKEN_TILE, width), lambda s: (jnp.clip(s - lag, 0, n_tiles - 1), 0))


def _params():
    return pltpu.CompilerParams(dimension_semantics=("arbitrary",), vmem_limit_bytes=V7X_VMEM_LIMIT_BYTES)


def _ffn1_call(x, pre_g, post_g, next_g, wg, wu, wd):
    n, d = x.shape
    n_tiles = n // TOKEN_TILE
    return pl.pallas_call(
        _ffn1_kernel,
        grid=(n_tiles + 2,),
        in_specs=[_tile_spec(d, n_tiles, 0), _tile_spec(d, n_tiles, 2)] + [_resident()] * 6,
        out_specs=[_tile_spec(d, n_tiles, 2), _tile_spec(d, n_tiles, 2)],
        out_shape=[jax.ShapeDtypeStruct((n, d), _F32), jax.ShapeDtypeStruct((n, d), _BF16)],
        scratch_shapes=[
            pltpu.VMEM((TOKEN_TILE, d), _BF16),
            pltpu.VMEM((TOKEN_TILE, d), _BF16),
            pltpu.VMEM((TOKEN_TILE, d), _F32),
            pltpu.VMEM((TOKEN_TILE, wd.shape[0]), _BF16),
        ],
        compiler_params=_params(),
        name="ffn1",
    )(x, x, pre_g, post_g, next_g, wg, wu, wd)


def _mixer_call(xn, h, seq, w_in, sgu_g, sgu_w, sgu_bt, pool_w, pool_scale, w_out_a, w_out_b, w_o,
                post_g, next_g):
    n, d = h.shape
    n_tiles = n // TOKEN_TILE
    return pl.pallas_call(
        functools.partial(_mixer_kernel, tiles_per_seq=seq // TOKEN_TILE),
        grid=(n_tiles + 1,),
        in_specs=[_tile_spec(d, n_tiles, 0), _tile_spec(d, n_tiles, 1)] + [_resident()] * 11,
        out_specs=[_tile_spec(d, n_tiles, 1), _tile_spec(d, n_tiles, 1)],
        out_shape=[jax.ShapeDtypeStruct((n, d), _F32), jax.ShapeDtypeStruct((n, d), _BF16)],
        scratch_shapes=[
            pltpu.VMEM((TOKEN_TILE, d), _BF16),
            pltpu.VMEM((TOKEN_TILE, d), _BF16),
            pltpu.VMEM((TOKEN_TILE + HALO, d), _F32),
            pltpu.VMEM((TOKEN_TILE, d), _F32),
            pltpu.VMEM((TOKEN_TILE, d), _F32),
            pltpu.VMEM((TOKEN_TILE, d), _F32),
            pltpu.VMEM((TOKEN_TILE, d), _F32),
            pltpu.VMEM((TOKEN_TILE, d), _F32),
        ],
        compiler_params=_params(),
        name="mixer",
    )(xn, h, w_in, sgu_g, sgu_w, sgu_bt, pool_w, pool_scale, w_out_a, w_out_b, w_o, post_g, next_g)


def _ffn2_ple_call(xn, h, p, post_g, wg, wu, wd, ple_pre_g, ple_wg, ple_wp, ple_post_g):
    n, d = h.shape
    n_tiles = n // TOKEN_TILE
    return pl.pallas_call(
        _ffn2_ple_kernel,
        grid=(n_tiles + 1,),
        in_specs=[_tile_spec(d, n_tiles, 0), _tile_spec(d, n_tiles, 1), _tile_spec(p.shape[1], n_tiles, 1)]
        + [_resident()] * 8,
        out_specs=_tile_spec(d, n_tiles, 1),
        out_shape=jax.ShapeDtypeStruct((n, d), _F32),
        scratch_shapes=[
            pltpu.VMEM((TOKEN_TILE, d), _F32),
            pltpu.VMEM((TOKEN_TILE, wd.shape[0]), _BF16),
            pltpu.VMEM((TOKEN_TILE, d), _F32),
            pltpu.VMEM((TOKEN_TILE, d), _BF16),
        ],
        compiler_params=_params(),
        name="ffn2_ple",
    )(xn, h, p, post_g, wg, wu, wd, ple_pre_g, ple_wg, ple_wp, ple_post_g)


def kernel(x, p, ffn1_pre_g, ffn1_w_gate, ffn1_w_up, ffn1_w_down, ffn1_post_g, mix_pre_g, w_in, sgu_norm_g, sgu_w, sgu_b, pool_w, pool_scale, w_out_a, w_out_b, w_o, mix_post_g, ffn2_pre_g, ffn2_w_gate, ffn2_w_up, ffn2_w_down, ffn2_post_g, ple_pre_g, ple_w_gate, ple_w_proj, ple_post_g):
    batch, seq, d = x.shape
    depth = p.shape[0]
    assert seq % TOKEN_TILE == 0 and TOKEN_TILE % CHUNK == 0
    bf = lambda w: w.astype(_BF16)
    row = lambda g: g.reshape(1, -1)

    h = x.reshape(batch * seq, d)
    for i in range(depth):
        h, xn = _ffn1_call(h, row(ffn1_pre_g[i]), row(ffn1_post_g[i]), row(mix_pre_g[i]),
                           bf(ffn1_w_gate[i]), bf(ffn1_w_up[i]), bf(ffn1_w_down[i]))
        h, xn = _mixer_call(xn, h, seq, bf(w_in[i]), row(sgu_norm_g[i]), sgu_w[i], sgu_b[i].T, bf(pool_w[i]),
                            row(pool_scale[i]), bf(w_out_a[i]), bf(w_out_b[i]), bf(w_o[i]),
                            row(mix_post_g[i]), row(ffn2_pre_g[i]))
        h = _ffn2_ple_call(xn, h, p[i].reshape(batch * seq, -1), row(ffn2_post_g[i]),
                           bf(ffn2_w_gate[i]), bf(ffn2_w_up[i]), bf(ffn2_w_down[i]),
                           row(ple_pre_g[i]), bf(ple_w_gate[i]), bf(ple_w_proj[i]), row(ple_post_g[i]))
    return h.reshape(batch, seq, d)
```

```python
import functools

import jax
import jax.numpy as jnp
from jax import lax
from jax.experimental import pallas as pl
from jax.experimental.pallas import tpu as pltpu

EPS = 1e-6
CHUNK = 128
N_SGU_GROUPS = 4
POOL_WINDOWS = (2, 4, 8, 16)
HALO = 16
TOKEN_TILE = 512
V7X_VMEM_LIMIT_BYTES = 58 * 1024 * 1024
FF_BLOCK = 256
ROW_CHUNK = 128
FFN_WEIGHT_BAND = 256
MIXER_WEIGHT_BAND = 128

_BF16 = jnp.bfloat16
_F32 = jnp.float32


def _rms(x, g):
    return x * lax.rsqrt(jnp.mean(x * x, axis=-1, keepdims=True) + EPS) * g


def _dot(a, b):
    return jnp.dot(a, b, preferred_element_type=_F32)


def _ordering_zero(x):
    rows, cols = x.shape
    acc = x[0:8, :]
    for r in range(1, rows // 8):
        acc = jnp.maximum(acc, x[8 * r:8 * r + 8, :])
    red = acc[:, 0:128]
    for k in range(1, cols // 128):
        red = jnp.maximum(red, acc[:, 128 * k:128 * (k + 1)])
    return jnp.minimum(jnp.abs(red[0:1, :]), 0.0)


def _widen(zero_row, width):
    return jnp.concatenate([zero_row] * (width // 128), axis=1)


def _swiglu_block(xn, wg_ref, wu_ref, act_scr, k, zero_row=None):
    cols = slice(k * FF_BLOCK, (k + 1) * FF_BLOCK)
    gate = _dot(xn, wg_ref[:, cols])
    up = _dot(xn, wu_ref[:, cols])
    started = jnp.minimum(jnp.abs(gate[0:1, 0:128]), 0.0)
    if zero_row is not None:
        gate = gate + _widen(zero_row, FF_BLOCK)
    act_scr[:, cols] = (jax.nn.silu(gate) * up).astype(_BF16)
    return started


def _n_ff_blocks(wg_ref):
    return wg_ref.shape[1] // FF_BLOCK


def _row_chunks(n_rows):
    return [slice(r, r + ROW_CHUNK) for r in range(0, n_rows, ROW_CHUNK)]


def _load_weights(pairs, stage_scr, sem):
    band_rows = stage_scr.shape[1]
    bands = []
    for hbm_ref, dst_scr in pairs:
        n_rows, n_cols = hbm_ref.shape
        assert n_rows % band_rows == 0 and n_cols <= stage_scr.shape[2]
        bands += [(hbm_ref, dst_scr, r0, n_cols) for r0 in range(0, n_rows, band_rows)]

    def band_copy(k):
        hbm_ref, _, r0, n_cols = bands[k]
        slot = k % 2
        return pltpu.make_async_copy(hbm_ref.at[pl.ds(r0, band_rows), :],
                                     stage_scr.at[slot, :, pl.ds(0, n_cols)], sem.at[slot])

    band_copy(0).start()
    for k, (_, dst_scr, r0, n_cols) in enumerate(bands):
        if k + 1 < len(bands):
            band_copy(k + 1).start()
        band_copy(k).wait()
        dst_scr[r0:r0 + band_rows, :] = stage_scr[k % 2, :, 0:n_cols].astype(_BF16)


def _ffn1_kernel(x_pre_ref, x_epi_ref, pre_g_ref, post_g_ref, next_g_ref, wg_hbm, wu_hbm, wd_hbm,
                 h_ref, xn_next_ref, xn_scr, xn_new_scr, f_scr, act_scr, wg_ref, wu_ref, wd_ref, stage_scr, sem):
    s = pl.program_id(0)
    last = pl.num_programs(0) - 1

    chunks = _row_chunks(x_pre_ref.shape[0])

    def prologue(rows):
        xn = _rms(x_pre_ref[rows, :], pre_g_ref[...])
        xn_new_scr[rows, :] = xn.astype(_BF16)
        return xn

    def epilogue(rows):
        h = x_epi_ref[rows, :] + 0.5 * _rms(f_scr[rows, :], post_g_ref[...])
        h_ref[rows, :] = h
        hn = _rms(h, next_g_ref[...])
        xn_next_ref[rows, :] = hn.astype(_BF16)
        return hn

    @pl.when(s == 0)
    def _():
        _load_weights([(wg_hbm, wg_ref), (wu_hbm, wu_ref), (wd_hbm, wd_ref)], stage_scr, sem)
        for rows in chunks:
            prologue(rows)
        xn_scr[...] = xn_new_scr[...]
        f_scr[...] = jnp.zeros(f_scr.shape, _F32)

    @pl.when((s > 0) & (s < last))
    def _():
        zero_rows = [None]
        for rows in chunks:
            zero_rows += [_ordering_zero(epilogue(rows)), _ordering_zero(prologue(rows))]
        zero_rows += [None] * (_n_ff_blocks(wg_ref) - len(zero_rows))
        xn = xn_scr[...]
        for k, zero_row in enumerate(zero_rows):
            _swiglu_block(xn, wg_ref, wu_ref, act_scr, k, zero_row)
        f_scr[...] = _dot(act_scr[...], wd_ref[...])
        xn_scr[...] = xn_new_scr[...]

    @pl.when(s == last)
    def _():
        for rows in chunks:
            epilogue(rows)


def _ffn2_ple_kernel(xn_ref, h_ref, p_ref, post_g_ref, wg_hbm, wu_hbm, wd_hbm,
                     ple_pre_g_ref, ple_wg_hbm, ple_wp_ref, ple_post_g_ref, o_ref,
                     f_scr, act_scr, h_scr, xn_ple_scr, wg_ref, wu_ref, wd_ref, ple_wg_ref, stage_scr, sem):
    s = pl.program_id(0)
    last = pl.num_programs(0) - 1

    chunks = _row_chunks(h_ref.shape[0])

    def residual(rows, started=None):
        post_g = post_g_ref[...]
        if started is not None:
            post_g = post_g + _widen(started, post_g.shape[1])
        h = h_ref[rows, :] + 0.5 * _rms(f_scr[rows, :], post_g)
        h_scr[rows, :] = h
        xn_ple_scr[rows, :] = _rms(h, ple_pre_g_ref[...]).astype(_BF16)

    def embed(rows, gate, e):
        out = h_scr[rows, :] + _rms(jax.nn.sigmoid(gate[rows, :]) * e[rows, :], ple_post_g_ref[...])
        o_ref[rows, :] = out
        return out

    def ple_matmuls():
        return _dot(xn_ple_scr[...], ple_wg_ref[...]), _dot(p_ref[...].astype(_BF16), ple_wp_ref[...])

    @pl.when(s == 0)
    def _():
        _load_weights([(wg_hbm, wg_ref), (wu_hbm, wu_ref), (wd_hbm, wd_ref), (ple_wg_hbm, ple_wg_ref)],
                      stage_scr, sem)
        f_scr[...] = jnp.zeros(f_scr.shape, _F32)

    @pl.when(s < last)
    def _():
        xn = xn_ref[...]
        n_blocks = _n_ff_blocks(wg_ref)
        for k, rows in enumerate(chunks):
            residual(rows, _swiglu_block(xn, wg_ref, wu_ref, act_scr, k))
        gate, e = ple_matmuls()
        zero_rows = [None] * (n_blocks - 2 * len(chunks))
        for rows in chunks:
            zero_rows.append(_ordering_zero(embed(rows, gate, e)))
        for k, zero_row in enumerate(zero_rows, start=len(chunks)):
            _swiglu_block(xn, wg_ref, wu_ref, act_scr, k, zero_row)
        f_scr[...] = _dot(act_scr[...], wd_ref[...])

    @pl.when(s == last)
    def _():
        for rows in chunks:
            residual(rows)
        gate, e = ple_matmuls()
        for rows in chunks:
            embed(rows, gate, e)


def _mixer_kernel(xn_ref, h_ref, w_in_hbm, sgu_g_ref, sgu_w_ref, sgu_bt_ref, pool_w_ref, pool_scale_ref,
                  w_out_a_hbm, w_out_b_hbm, w_o_hbm, post_g_ref, next_g_ref,
                  h_out_ref, xn_next_ref, a_scr, b_scr, cext_scr, zu_scr, zv_scr, zga_scr, zgb_scr, m_scr,
                  w_in_ref, w_out_a_ref, w_out_b_ref, w_o_ref, stage_scr, sem, *, tiles_per_seq):
    tm, d = xn_ref.shape
    dg = d // N_SGU_GROUPS
    s = pl.program_id(0)
    last = pl.num_programs(0) - 1

    def epilogue(rows):
        h = h_ref[rows, :] + _rms(m_scr[rows, :], post_g_ref[...])
        h_out_ref[rows, :] = h
        hn = _rms(h, next_g_ref[...])
        xn_next_ref[rows, :] = hn.astype(_BF16)
        return hn

    def mixer():
        j = lax.rem(s, tiles_per_seq)
        xn = xn_ref[...]
        chunks = _row_chunks(tm)

        @pl.when(j == 0)
        def _():
            cext_scr[0:HALO, :] = jnp.zeros((HALO, d), _F32)

        zv_scr[...] = _dot(xn, w_in_ref[:, d:2 * d])
        for rows in chunks:
            epilogue(rows)
        cext_scr[HALO:HALO + tm, :] = _dot(xn, w_in_ref[:, 2 * d:3 * d])
        zu_scr[...] = _dot(xn, w_in_ref[:, 0:d])

        v = jax.nn.gelu(zv_scr[...])
        mu = jnp.mean(v, axis=-1, keepdims=True)
        vc = v - mu
        v = (vc * lax.rsqrt(jnp.mean(vc * vc, axis=-1, keepdims=True) + EPS) * sgu_g_ref[...]).astype(_BF16)

        c = cext_scr[HALO:HALO + tm, :]
        t = j * tm + lax.broadcasted_iota(jnp.int32, (tm, 1), 0)
        dp = d // len(POOL_WINDOWS)
        diffs = []
        for gi, w in enumerate(POOL_WINDOWS):
            cs = slice(gi * dp, (gi + 1) * dp)
            acc = cext_scr[:, cs]
            span = 1
            while span < w:
                acc = acc + pltpu.roll(acc, span, axis=0)
                span *= 2
            count = jnp.minimum(t + 1, w).astype(_F32)
            diffs.append(acc[HALO:HALO + tm] / count - c[:, cs])
        cext_scr[0:HALO, :] = cext_scr[tm:tm + HALO, :]
        zgb_scr[...] = _dot(xn, w_in_ref[:, 4 * d:5 * d])

        u = jax.nn.gelu(zu_scr[...])
        zga_scr[...] = _dot(xn, w_in_ref[:, 3 * d:4 * d])

        row = lax.broadcasted_iota(jnp.int32, (CHUNK, CHUNK), 0)
        col = lax.broadcasted_iota(jnp.int32, (CHUNK, CHUNK), 1)
        for g in range(N_SGU_GROUPS):
            ws = jnp.where(row >= col, sgu_w_ref[g], 0.0).astype(_BF16)
            bias = sgu_bt_ref[:, g:g + 1]
            for ci in range(tm // CHUNK):
                rs = slice(ci * CHUNK, (ci + 1) * CHUNK)
                cs = slice(g * dg, (g + 1) * dg)
                sv = _dot(ws, v[rs, cs]) + bias
                a_scr[rs, cs] = (u[rs, cs] * sv).astype(_BF16)

        for gi, diff in enumerate(diffs):
            cs = slice(gi * dp, (gi + 1) * dp)
            b_scr[:, cs] = (_dot(diff.astype(_BF16), pool_w_ref[gi]) * pool_scale_ref[:, cs]).astype(_BF16)

        yb = jax.nn.sigmoid(zgb_scr[...]) * _dot(b_scr[...], w_out_b_ref[...])
        y = jax.nn.sigmoid(zga_scr[...]) * _dot(a_scr[...], w_out_a_ref[...]) + yb
        m_scr[...] = _dot(y.astype(_BF16), w_o_ref[...])

    @pl.when(s == 0)
    def _():
        _load_weights([(w_in_hbm, w_in_ref), (w_out_a_hbm, w_out_a_ref), (w_out_b_hbm, w_out_b_ref),
                       (w_o_hbm, w_o_ref)], stage_scr, sem)
        m_scr[...] = jnp.zeros(m_scr.shape, _F32)

    @pl.when(s < last)
    def _():
        mixer()

    @pl.when(s == last)
    def _():
        for rows in _row_chunks(tm):
            epilogue(rows)


def _resident():
    return pl.BlockSpec(memory_space=pltpu.VMEM)


def _in_hbm():
    return pl.BlockSpec(memory_space=pl.ANY)


def _weight_scratch(weights, band_rows):
    max_cols = max(w.shape[1] for w in weights)
    return ([pltpu.VMEM(w.shape, _BF16) for w in weights]
            + [pltpu.VMEM((2, band_rows, max_cols), _F32), pltpu.SemaphoreType.DMA((2,))])


def _tile_spec(width, n_tiles, lag):
    return pl.BlockSpec((TOKEN_TILE, width), lambda s: (jnp.clip(s - lag, 0, n_tiles - 1), 0))


def _params():
    return pltpu.CompilerParams(dimension_semantics=("arbitrary",), vmem_limit_bytes=V7X_VMEM_LIMIT_BYTES)


def _ffn1_call(x, pre_g, post_g, next_g, wg, wu, wd):
    n, d = x.shape
    n_tiles = n // TOKEN_TILE
    return pl.pallas_call(
        _ffn1_kernel,
        grid=(n_tiles + 2,),
        in_specs=[_tile_spec(d, n_tiles, 0), _tile_spec(d, n_tiles, 2)] + [_resident()] * 3 + [_in_hbm()] * 3,
        out_specs=[_tile_spec(d, n_tiles, 2), _tile_spec(d, n_tiles, 2)],
        out_shape=[jax.ShapeDtypeStruct((n, d), _F32), jax.ShapeDtypeStruct((n, d), _BF16)],
        scratch_shapes=[
            pltpu.VMEM((TOKEN_TILE, d), _BF16),
            pltpu.VMEM((TOKEN_TILE, d), _BF16),
            pltpu.VMEM((TOKEN_TILE, d), _F32),
            pltpu.VMEM((TOKEN_TILE, wd.shape[0]), _BF16),
        ] + _weight_scratch([wg, wu, wd], FFN_WEIGHT_BAND),
        compiler_params=_params(),
        name="ffn1",
    )(x, x, pre_g, post_g, next_g, wg, wu, wd)


def _mixer_call(xn, h, seq, w_in, sgu_g, sgu_w, sgu_bt, pool_w, pool_scale, w_out_a, w_out_b, w_o,
                post_g, next_g):
    n, d = h.shape
    n_tiles = n // TOKEN_TILE
    return pl.pallas_call(
        functools.partial(_mixer_kernel, tiles_per_seq=seq // TOKEN_TILE),
        grid=(n_tiles + 1,),
        in_specs=[_tile_spec(d, n_tiles, 0), _tile_spec(d, n_tiles, 1), _in_hbm()] + [_resident()] * 5
        + [_in_hbm()] * 3 + [_resident()] * 2,
        out_specs=[_tile_spec(d, n_tiles, 1), _tile_spec(d, n_tiles, 1)],
        out_shape=[jax.ShapeDtypeStruct((n, d), _F32), jax.ShapeDtypeStruct((n, d), _BF16)],
        scratch_shapes=[
            pltpu.VMEM((TOKEN_TILE, d), _BF16),
            pltpu.VMEM((TOKEN_TILE, d), _BF16),
            pltpu.VMEM((TOKEN_TILE + HALO, d), _F32),
            pltpu.VMEM((TOKEN_TILE, d), _F32),
            pltpu.VMEM((TOKEN_TILE, d), _F32),
            pltpu.VMEM((TOKEN_TILE, d), _F32),
            pltpu.VMEM((TOKEN_TILE, d), _F32),
            pltpu.VMEM((TOKEN_TILE, d), _F32),
        ] + _weight_scratch([w_in, w_out_a, w_out_b, w_o], MIXER_WEIGHT_BAND),
        compiler_params=_params(),
        name="mixer",
    )(xn, h, w_in, sgu_g, sgu_w, sgu_bt, pool_w, pool_scale, w_out_a, w_out_b, w_o, post_g, next_g)


def _ffn2_ple_call(xn, h, p, post_g, wg, wu, wd, ple_pre_g, ple_wg, ple_wp, ple_post_g):
    n, d = h.shape
    n_tiles = n // TOKEN_TILE
    return pl.pallas_call(
        _ffn2_ple_kernel,
        grid=(n_tiles + 1,),
        in_specs=[_tile_spec(d, n_tiles, 0), _tile_spec(d, n_tiles, 1), _tile_spec(p.shape[1], n_tiles, 1)]
        + [_resident()] + [_in_hbm()] * 3 + [_resident(), _in_hbm(), _resident(), _resident()],
        out_specs=_tile_spec(d, n_tiles, 1),
        out_shape=jax.ShapeDtypeStruct((n, d), _F32),
        scratch_shapes=[
            pltpu.VMEM((TOKEN_TILE, d), _F32),
            pltpu.VMEM((TOKEN_TILE, wd.shape[0]), _BF16),
            pltpu.VMEM((TOKEN_TILE, d), _F32),
            pltpu.VMEM((TOKEN_TILE, d), _BF16),
        ] + _weight_scratch([wg, wu, wd, ple_wg], FFN_WEIGHT_BAND),
        compiler_params=_params(),
        name="ffn2_ple",
    )(xn, h, p, post_g, wg, wu, wd, ple_pre_g, ple_wg, ple_wp, ple_post_g)


def kernel(x, p, ffn1_pre_g, ffn1_w_gate, ffn1_w_up, ffn1_w_down, ffn1_post_g, mix_pre_g, w_in, sgu_norm_g, sgu_w, sgu_b, pool_w, pool_scale, w_out_a, w_out_b, w_o, mix_post_g, ffn2_pre_g, ffn2_w_gate, ffn2_w_up, ffn2_w_down, ffn2_post_g, ple_pre_g, ple_w_gate, ple_w_proj, ple_post_g):
    batch, seq, d = x.shape
    depth = p.shape[0]
    assert seq % TOKEN_TILE == 0 and TOKEN_TILE % CHUNK == 0
    bf = lambda w: w.astype(_BF16)
    row = lambda g: g.reshape(1, -1)

    h = x.reshape(batch * seq, d)
    for i in range(depth):
        h, xn = _ffn1_call(h, row(ffn1_pre_g[i]), row(ffn1_post_g[i]), row(mix_pre_g[i]),
                           ffn1_w_gate[i], ffn1_w_up[i], ffn1_w_down[i])
        h, xn = _mixer_call(xn, h, seq, w_in[i], row(sgu_norm_g[i]), sgu_w[i], sgu_b[i].T, bf(pool_w[i]),
                            row(pool_scale[i]), w_out_a[i], w_out_b[i], w_o[i],
                            row(mix_post_g[i]), row(ffn2_pre_g[i]))
        h = _ffn2_ple_call(xn, h, p[i].reshape(batch * seq, -1), row(ffn2_post_g[i]),
                           ffn2_w_gate[i], ffn2_w_up[i], ffn2_w_down[i],
                           row(ple_pre_g[i]), ple_w_gate[i], bf(ple_w_proj[i]), row(ple_post_g[i]))
    return h.reshape(batch, seq, d)
```
